```python
import math
import jax
import jax.numpy as jnp
from jax import lax
import numpy as np

D_MODEL = 2048
BATCH = 2
SEQ = 16384
DEPTH = 1
DEC_BATCH = 4
DEC_SEQ = 2048
PAST_LEN = 128

GRID_W = 64
NA_DH = 128
NA_HEADS = D_MODEL // NA_DH
NA_W = NA_HEADS * NA_DH
NA_KH_MAX = 8
NA_KW = 16
NA_SCALE = NA_DH ** -0.5

MLA_HEADS = D_MODEL // 128
Q_LORA = D_MODEL // 4
KV_LORA = D_MODEL // 4
NOPE_DIM = 128
ROPE_DIM = 64
V_DIM = 128
ROPE_THETA = 10000.0
MLA_SCALE = (NOPE_DIM + ROPE_DIM) ** -0.5
Q_BLOCK = 128

N_EXPERTS = 32
TOP_K = 4
D_FF_EXPERT = D_MODEL
SWIGLU_LIMIT = 7.0
SWIGLU_ALPHA = 1.702
MOE_BLOCK = 128

PLE_DIM = 256
RMS_EPS = 1e-6

IN_WIDTHS = (NA_W, NA_W, NA_W, Q_LORA, KV_LORA, ROPE_DIM, D_MODEL, D_MODEL)
N_IN = 3 * NA_W + Q_LORA + KV_LORA + ROPE_DIM + 2 * D_MODEL

kernel_name = "hybrid_natten_mla_moe_encoder"


def rmsnorm(x, g):
    xf = x.astype(jnp.float32)
    y = xf * lax.rsqrt(jnp.mean(xf * xf, axis=-1, keepdims=True) + RMS_EPS)
    return (y * g.astype(jnp.float32)).astype(x.dtype)


def split_cols(z):
    outs, start = [], 0
    for w in IN_WIDTHS:
        outs.append(z[..., start:start + w])
        start += w
    return outs


def rope_tables(length, dtype):
    pos = jnp.arange(length, dtype=jnp.float32)
    inv_freq = ROPE_THETA ** (-jnp.arange(0, ROPE_DIM, 2, dtype=jnp.float32) / ROPE_DIM)
    ang = pos[:, None] * inv_freq[None, :]
    return jnp.cos(ang).astype(dtype), jnp.sin(ang).astype(dtype)


def apply_rope(x, cos, sin):
    x1, x2 = x[..., :ROPE_DIM // 2], x[..., ROPE_DIM // 2:]
    return jnp.concatenate([x1 * cos - x2 * sin, x2 * cos + x1 * sin], axis=-1)


def neighbourhood_attention(q, k, v, rpb):
    B, L, H, DH = q.shape
    rows = L // GRID_W
    kh = min(NA_KH_MAX, rows)
    kg = k.reshape(B, rows, GRID_W, H, DH)
    vg = v.reshape(B, rows, GRID_W, H, DH)
    qg = q.reshape(B, rows, GRID_W, H, DH).swapaxes(0, 1)
    cols = jnp.arange(GRID_W)
    col_start = jnp.clip(cols - NA_KW // 2, 0, GRID_W - NA_KW)
    col_idx = col_start[:, None] + jnp.arange(NA_KW)[None, :]
    dx = col_idx - cols[:, None] + (NA_KW - 1)
    rpb_f = rpb.astype(jnp.float32)

    def row_block(args):
        r, qr = args
        rs = jnp.clip(r - kh // 2, 0, rows - kh)
        kb = lax.dynamic_slice_in_dim(kg, rs, kh, axis=1)
        vb = lax.dynamic_slice_in_dim(vg, rs, kh, axis=1)
        ks = kb[:, :, col_idx]
        vs = vb[:, :, col_idx]
        dy = rs + jnp.arange(kh) - r + (NA_KH_MAX - 1)
        bias = rpb_f[:, dy[:, None, None], dx[None, :, :]]
        bias = bias.transpose(0, 2, 1, 3)
        s = jnp.einsum('bqhd,byqxhd->bhqyx', qr, ks).astype(jnp.float32) * NA_SCALE + bias[None]
        w = jax.nn.softmax(s.reshape(B, H, GRID_W, kh * NA_KW), axis=-1)
        w = w.reshape(B, H, GRID_W, kh, NA_KW).astype(v.dtype)
        return jnp.einsum('bhqyx,byqxhd->bqhd', w, vs)

    out = lax.map(row_block, (jnp.arange(rows), qg))
    return out.swapaxes(0, 1).reshape(B, L, H * DH)


def latent_attention(q_nope, q_rope, k_nope, k_rope, v):
    B, L, H, _ = q_nope.shape
    nb = L // Q_BLOCK
    qn = q_nope.reshape(B, nb, Q_BLOCK, H, NOPE_DIM).swapaxes(0, 1)
    qr = q_rope.reshape(B, nb, Q_BLOCK, H, ROPE_DIM).swapaxes(0, 1)

    def q_block(args):
        qnb, qrb = args
        s = (jnp.einsum('bqhd,bkhd->bhqk', qnb, k_nope)
             + jnp.einsum('bqhr,bkr->bhqk', qrb, k_rope))
        w = jax.nn.softmax(s.astype(jnp.float32) * MLA_SCALE, axis=-1).astype(v.dtype)
        return jnp.einsum('bhqk,bkhd->bqhd', w, v)

    out = lax.map(q_block, (qn, qr))
    return out.swapaxes(0, 1).reshape(B, L, H * V_DIM)


def moe_ffn(h, w_router, b_router, w_gate, b_gate, w_up, b_up, w_down, b_down):
    T, D = h.shape
    logits = (h @ w_router + b_router).astype(jnp.float32)
    top_val, top_idx = lax.top_k(logits, TOP_K)
    gates = jax.nn.softmax(top_val, axis=-1)
    TK = T * TOP_K
    flat_e = top_idx.reshape(TK).astype(jnp.int32)
    flat_tok = jnp.repeat(jnp.arange(T, dtype=jnp.int32), TOP_K)
    flat_g = gates.reshape(TK)
    order = jnp.argsort(flat_e, stable=True)
    se = flat_e[order]
    counts = jnp.bincount(flat_e, length=N_EXPERTS).astype(jnp.int32)
    pcounts = (counts + MOE_BLOCK - 1) // MOE_BLOCK * MOE_BLOCK
    start = jnp.cumsum(counts) - counts
    pend = jnp.cumsum(pcounts)
    pstart = pend - pcounts
    dest = pstart[se] + jnp.arange(TK, dtype=jnp.int32) - start[se]
    n_blocks = (TK + N_EXPERTS * (MOE_BLOCK - 1) + MOE_BLOCK - 1) // MOE_BLOCK
    P = n_blocks * MOE_BLOCK
    row_tok = jnp.zeros((P,), jnp.int32).at[dest].set(flat_tok[order])
    row_gate = jnp.zeros((P,), jnp.float32).at[dest].set(flat_g[order])
    block_e = jnp.minimum(
        jnp.searchsorted(pend, jnp.arange(n_blocks, dtype=jnp.int32) * MOE_BLOCK, side='right'),
        N_EXPERTS - 1)
    xs = h[row_tok].reshape(n_blocks, MOE_BLOCK, D)

    def expert_block(args):
        xb, e = args
        g = xb @ w_gate[e] + b_gate[e]
        u = xb @ w_up[e] + b_up[e]
        g = jnp.minimum(g, SWIGLU_LIMIT)
        u = jnp.clip(u, -SWIGLU_LIMIT, SWIGLU_LIMIT)
        a = g * jax.nn.sigmoid(SWIGLU_ALPHA * g) * (u + 1.0)
        return a @ w_down[e] + b_down[e]

    ys = lax.map(expert_block, (xs, block_e)).reshape(P, D)
    ys = ys * row_gate[:, None].astype(ys.dtype)
    return jnp.zeros((T, D), h.dtype).at[row_tok].add(ys)


def encode(x, p, g_mix, w_in, g_q_lat, w_q_up, g_kv_lat, w_kv_up, na_rpb, w_br_a, w_br_b,
           w_out, g_moe, w_router, b_router, w_gate, b_gate, w_up, b_up, w_down, b_down,
           g_ple, w_ple_gate, w_ple_proj, g_final):
    B, L, D = x.shape
    cos, sin = rope_tables(L, x.dtype)
    for i in range(DEPTH):
        h = rmsnorm(x, g_mix[i])
        z = h @ w_in[i]
        qa, ka, va, cq, ckv, kr, ga, gb = split_cols(z)
        ya = neighbourhood_attention(qa.reshape(B, L, NA_HEADS, NA_DH),
                                     ka.reshape(B, L, NA_HEADS, NA_DH),
                                     va.reshape(B, L, NA_HEADS, NA_DH), na_rpb[i])
        ya = ya @ w_br_a[i]
        q = (rmsnorm(cq, g_q_lat[i]) @ w_q_up[i]).reshape(B, L, MLA_HEADS, NOPE_DIM + ROPE_DIM)
        q_nope = q[..., :NOPE_DIM]
        q_rope = apply_rope(q[..., NOPE_DIM:], cos[None, :, None, :], sin[None, :, None, :])
        kv = (rmsnorm(ckv, g_kv_lat[i]) @ w_kv_up[i]).reshape(B, L, MLA_HEADS, NOPE_DIM + V_DIM)
        k_nope, v = kv[..., :NOPE_DIM], kv[..., NOPE_DIM:]
        k_rope = apply_rope(kr, cos[None], sin[None])
        yb = latent_attention(q_nope, q_rope, k_nope, k_rope, v) @ w_br_b[i]
        mix = (jax.nn.sigmoid(ga) * ya + jax.nn.sigmoid(gb) * yb) @ w_out[i]
        x = x + mix
        h2 = rmsnorm(x, g_moe[i]).reshape(B * L, D)
        x = x + moe_ffn(h2, w_router[i], b_router[i], w_gate[i], b_gate[i], w_up[i], b_up[i],
                        w_down[i], b_down[i]).reshape(B, L, D)
        h3 = rmsnorm(x, g_ple[i])
        x = x + jax.nn.sigmoid(h3 @ w_ple_gate[i]) * (p[i] @ w_ple_proj[i])
    return rmsnorm(x, g_final)


def setup_inputs(seed: int = 0) -> dict:
    key = jax.random.key(seed)
    ks = jax.random.split(key, 32)
    f32 = jnp.float32

    def nrm(k, shape, scale):
        return jax.random.normal(k, shape, f32) * scale

    def gain(k, shape):
        return 1.0 + 0.05 * jax.random.normal(k, shape, f32)

    return {
        "x_prompt": nrm(ks[0], (BATCH, SEQ, D_MODEL), 1.0),
        "x_sample": nrm(ks[1], (DEC_BATCH, DEC_SEQ, D_MODEL), 1.0),
        "p_prompt": nrm(ks[2], (DEPTH, BATCH, SEQ, PLE_DIM), 1.0),
        "p_sample": nrm(ks[3], (DEPTH, DEC_BATCH, DEC_SEQ, PLE_DIM), 1.0),
        "g_mix": gain(ks[4], (DEPTH, D_MODEL)),
        "w_in": nrm(ks[5], (DEPTH, D_MODEL, N_IN), D_MODEL ** -0.5),
        "g_q_lat": gain(ks[6], (DEPTH, Q_LORA)),
        "w_q_up": nrm(ks[7], (DEPTH, Q_LORA, MLA_HEADS * (NOPE_DIM + ROPE_DIM)), Q_LORA ** -0.5),
        "g_kv_lat": gain(ks[8], (DEPTH, KV_LORA)),
        "w_kv_up": nrm(ks[9], (DEPTH, KV_LORA, MLA_HEADS * (NOPE_DIM + V_DIM)), KV_LORA ** -0.5),
        "na_rpb": nrm(ks[10], (DEPTH, NA_HEADS, 2 * NA_KH_MAX - 1, 2 * NA_KW - 1), 0.1),
        "w_br_a": nrm(ks[11], (DEPTH, NA_W, D_MODEL), NA_W ** -0.5),
        "w_br_b": nrm(ks[12], (DEPTH, MLA_HEADS * V_DIM, D_MODEL), (MLA_HEADS * V_DIM) ** -0.5),
        "w_out": nrm(ks[13], (DEPTH, D_MODEL, D_MODEL), D_MODEL ** -0.5),
        "g_moe": gain(ks[14], (DEPTH, D_MODEL)),
        "w_router": nrm(ks[15], (DEPTH, D_MODEL, N_EXPERTS), D_MODEL ** -0.5),
        "b_router": nrm(ks[16], (DEPTH, N_EXPERTS), 0.01),
        "w_gate": nrm(ks[17], (DEPTH, N_EXPERTS, D_MODEL, D_FF_EXPERT), D_MODEL ** -0.5),
        "b_gate": nrm(ks[18], (DEPTH, N_EXPERTS, D_FF_EXPERT), 0.01),
        "w_up": nrm(ks[19], (DEPTH, N_EXPERTS, D_MODEL, D_FF_EXPERT), D_MODEL ** -0.5),
        "b_up": nrm(ks[20], (DEPTH, N_EXPERTS, D_FF_EXPERT), 0.01),
        "w_down": nrm(ks[21], (DEPTH, N_EXPERTS, D_FF_EXPERT, D_MODEL), D_FF_EXPERT ** -0.5),
        "b_down": nrm(ks[22], (DEPTH, N_EXPERTS, D_MODEL), 0.01),
        "g_ple": gain(ks[23], (DEPTH, D_MODEL)),
        "w_ple_gate": nrm(ks[24], (DEPTH, D_MODEL, D_MODEL), D_MODEL ** -0.5),
        "w_ple_proj": nrm(ks[25], (DEPTH, PLE_DIM, D_MODEL), PLE_DIM ** -0.5),
        "g_final": gain(ks[26], (D_MODEL,)),
    }


def reference(x_prompt, x_sample, p_prompt, p_sample, g_mix, w_in, g_q_lat, w_q_up, g_kv_lat,
              w_kv_up, na_rpb, w_br_a, w_br_b, w_out, g_moe, w_router, b_router, w_gate, b_gate,
              w_up, b_up, w_down, b_down, g_ple, w_ple_gate, w_ple_proj, g_final):
    y_prompt = encode(x_prompt, p_prompt, g_mix, w_in, g_q_lat, w_q_up, g_kv_lat, w_kv_up, na_rpb,
                      w_br_a, w_br_b, w_out, g_moe, w_router, b_router, w_gate, b_gate, w_up, b_up,
                      w_down, b_down, g_ple, w_ple_gate, w_ple_proj, g_final)
    y_sample = encode(x_sample, p_sample, g_mix, w_in, g_q_lat, w_q_up, g_kv_lat, w_kv_up, na_rpb,
                      w_br_a, w_br_b, w_out, g_moe, w_router, b_router, w_gate, b_gate, w_up, b_up,
                      w_down, b_down, g_ple, w_ple_gate, w_ple_proj, g_final)
    return (y_prompt, y_sample)
```

```python
import functools

import jax
import jax.numpy as jnp
from jax import lax
from jax.experimental import pallas as pl
from jax.experimental.pallas import tpu as pltpu

F32 = jnp.float32
BF16 = jnp.bfloat16
I32 = jnp.int32

LANE = 128
VMEM_LIMIT_BYTES = 56 * 1024 * 1024

RMS_EPS = 1e-6
GRID_W = 64
NA_KH = 8
NA_KW = 16
NA_DH = 128
NA_QROWS = 8
NA_QTOK = NA_QROWS * GRID_W
NA_WTOK = 2 * NA_QTOK
NA_PIECE = NA_WTOK // 4
NOPE = 128
ROPE = 64
V_DIM = 128
HEAD_PAD = 256
ROPE_THETA = 10000.0
TOP_K = 4
SWIGLU_LIMIT = 7.0
SWIGLU_ALPHA = 1.702
MOE_BM = 512
NEG = -1e30


def _cparams(*sem):
    return pltpu.CompilerParams(dimension_semantics=sem, vmem_limit_bytes=VMEM_LIMIT_BYTES)


def _pick(n, cands):
    for c in cands:
        if n % c == 0:
            return c
    raise ValueError(f"no tile in {cands} divides {n}")


def _resident(shape):
    nd = len(shape)
    return pl.BlockSpec(shape, lambda *_: (0,) * nd, pipeline_mode=pl.Buffered(1))


def _rms(x, g):
    return x * lax.rsqrt(jnp.mean(x * x, axis=-1, keepdims=True) + RMS_EPS) * g


def _sigmoid(x):
    return 1.0 / (1.0 + jnp.exp(-x))


def _inproj_kernel(x_ref, g_ref, w_ref, wkr_ref, z_ref, kr_ref, h_scr):
    @pl.when(pl.program_id(1) == 0)
    def _():
        hb = _rms(x_ref[...], g_ref[...]).astype(BF16)
        h_scr[...] = hb
        kr_ref[...] = jnp.dot(hb, wkr_ref[...], preferred_element_type=F32)

    z_ref[...] = jnp.dot(h_scr[...], w_ref[...], preferred_element_type=F32).astype(z_ref.dtype)


def _inproj(x, g, w, wkr):
    T, D = x.shape
    N = w.shape[1]
    tm = _pick(T, (1024, 512, 256))
    tn = _pick(N, (1024, 512, 256, 128))
    return pl.pallas_call(
        _inproj_kernel,
        grid=(T // tm, N // tn),
        in_specs=[pl.BlockSpec((tm, D), lambda i, j: (i, 0)),
                  pl.BlockSpec((1, D), lambda i, j: (0, 0)),
                  pl.BlockSpec((D, tn), lambda i, j: (0, j)),
                  pl.BlockSpec((D, LANE), lambda i, j: (0, 0))],
        out_specs=[pl.BlockSpec((tm, tn), lambda i, j: (i, j)),
                   pl.BlockSpec((tm, LANE), lambda i, j: (i, 0))],
        out_shape=[jax.ShapeDtypeStruct((T, N), BF16), jax.ShapeDtypeStruct((T, LANE), F32)],
        scratch_shapes=[pltpu.VMEM((tm, D), BF16)],
        compiler_params=_cparams("parallel", "arbitrary"),
        name="inproj",
    )(x, g, w, wkr)


def _mlaproj_kernel(cq_ref, ckv_ref, kr_ref, cos_ref, sin_ref, gq_ref, gkv_ref, wq_ref, wkv_ref,
                    q_out, k_out, v_out, *, heads):
    cos = cos_ref[...]
    sin = sin_ref[...]

    def rope(b):
        return b * cos + pltpu.roll(b, ROPE, axis=1) * sin

    q = jnp.dot(_rms(cq_ref[...].astype(F32), gq_ref[...]).astype(BF16), wq_ref[...],
                preferred_element_type=F32)
    kv = jnp.dot(_rms(ckv_ref[...].astype(F32), gkv_ref[...]).astype(BF16), wkv_ref[...],
                 preferred_element_type=F32)
    krope = rope(kr_ref[...]).astype(BF16)
    ones = jnp.ones(krope.shape, BF16)
    for h in range(heads):
        c = h * HEAD_PAD
        q_out[:, c:c + NOPE] = q[:, c:c + NOPE].astype(BF16)
        q_out[:, c + NOPE:c + HEAD_PAD] = rope(q[:, c + NOPE:c + HEAD_PAD]).astype(BF16)
        k_out[:, c:c + NOPE] = kv[:, c:c + NOPE].astype(BF16)
        k_out[:, c + NOPE:c + HEAD_PAD] = krope
        v_out[:, c:c + V_DIM] = kv[:, c + NOPE:c + HEAD_PAD].astype(BF16)
        v_out[:, c + V_DIM:c + HEAD_PAD] = ones


def _mlaproj(z, kr, cos_t, sin_t, gq, gkv, wq, wkv, *, cq_col, groups, heads):
    T = z.shape[0]
    ql = gq.shape[-1]
    kvl = gkv.shape[-1]
    W = heads * HEAD_PAD
    (bp, lp), (bs, ls) = groups
    tm = _pick(ls, (256, 128))
    n_p = bp * lp // tm

    def pos_map(i):
        return (jnp.where(i < n_p, i % (lp // tm), (i - n_p) % (ls // tm)), 0)

    out = jax.ShapeDtypeStruct((T, W), BF16)
    return pl.pallas_call(
        functools.partial(_mlaproj_kernel, heads=heads),
        grid=(T // tm,),
        in_specs=[pl.BlockSpec((tm, ql), lambda i: (i, cq_col // ql)),
                  pl.BlockSpec((tm, kvl), lambda i: (i, (cq_col + ql) // kvl)),
                  pl.BlockSpec((tm, LANE), lambda i: (i, 0)),
                  pl.BlockSpec((tm, LANE), pos_map),
                  pl.BlockSpec((tm, LANE), pos_map),
                  _resident((1, ql)), _resident((1, kvl)),
                  _resident((ql, W)), _resident((kvl, W))],
        out_specs=[pl.BlockSpec((tm, W), lambda i: (i, 0))] * 3,
        out_shape=[out, out, out],
        compiler_params=_cparams("parallel"),
        name="mlaproj",
    )(z, z, kr, cos_t, sin_t, gq, gkv, wq, wkv)


def _mla_attn_kernel(q_ref, k_ref, v_ref, o_ref, m_scr, acc_scr, *, tk, nk):
    q = q_ref[...]
    m_scr[...] = jnp.full(m_scr.shape, NEG, F32)
    acc_scr[...] = jnp.zeros(acc_scr.shape, F32)

    def body(c, carry):
        off = pl.multiple_of(c * tk, tk)
        s = lax.dot_general(q, k_ref[pl.ds(off, tk), :], (((1,), (1,)), ((), ())),
                            preferred_element_type=F32)
        m_prev = m_scr[...]
        m_new = jnp.maximum(m_prev, jnp.max(s, axis=-1, keepdims=True))
        alpha = jnp.exp(m_prev - m_new)
        p = jnp.exp(s - m_new).astype(BF16)
        acc_scr[...] = acc_scr[...] * alpha + jnp.dot(p, v_ref[pl.ds(off, tk), :],
                                                      preferred_element_type=F32)
        m_scr[...] = m_new
        return carry

    lax.fori_loop(0, nk, body, 0)
    acc = acc_scr[...]
    o_ref[...] = (acc[:, :V_DIM] / acc[:, V_DIM:V_DIM + 1]).astype(o_ref.dtype)


def _mla_attn(qp, kp, vp, *, batch, length, tok_off, heads):
    tq = _pick(length, (1024, 512))
    tk = _pick(length, (512,))
    nq = length // tq
    qoff = tok_off // tq
    boff = tok_off // length
    return pl.pallas_call(
        functools.partial(_mla_attn_kernel, tk=tk, nk=length // tk),
        grid=(batch, heads, nq),
        in_specs=[pl.BlockSpec((tq, HEAD_PAD), lambda b, h, i: (qoff + b * nq + i, h)),
                  pl.BlockSpec((length, HEAD_PAD), lambda b, h, i: (boff + b, h)),
                  pl.BlockSpec((length, HEAD_PAD), lambda b, h, i: (boff + b, h))],
        out_specs=pl.BlockSpec((tq, V_DIM), lambda b, h, i: (b * nq + i, h)),
        out_shape=jax.ShapeDtypeStruct((batch * length, heads * V_DIM), BF16),
        scratch_shapes=[pltpu.VMEM((tq, 1), F32), pltpu.VMEM((tq, HEAD_PAD), F32)],
        compiler_params=_cparams("parallel", "parallel", "arbitrary"),
        name="mla_attn",
    )(qp, kp, vp)


def _na_kernel(q_ref, k0, k1, k2, k3, v0, v1, v2, v3, b_ref, o_ref):
    q = q_ref[...]
    pw = NA_PIECE
    ss = []
    for j, kr in enumerate((k0, k1, k2, k3)):
        s = lax.dot_general(q, kr[...], (((1,), (1,)), ((), ())), preferred_element_type=F32)
        ss.append(s + b_ref[:, j * pw:(j + 1) * pw])
    m = functools.reduce(jnp.maximum, [jnp.max(s, axis=-1, keepdims=True) for s in ss])
    acc = None
    den = None
    for s, vr in zip(ss, (v0, v1, v2, v3)):
        p = jnp.exp(s - m)
        d = jnp.sum(p, axis=-1, keepdims=True)
        a = jnp.dot(p.astype(BF16), vr[...], preferred_element_type=F32)
        acc = a if acc is None else acc + a
        den = d if den is None else den + d
    o_ref[...] = (acc / den).astype(o_ref.dtype)


def _na_bias(rpb):
    H = rpb.shape[0]
    a = jnp.arange(NA_QROWS)
    j = jnp.arange(2 * NA_QROWS)
    c = jnp.arange(GRID_W)
    q_rel = jnp.stack([a, a + NA_KH // 2, a + NA_QROWS])
    rs_rel = jnp.stack([jnp.maximum(a - NA_KH // 2, 0), a, jnp.minimum(a + NA_KH // 2, NA_QROWS)])
    vrow = (j[None, None, :] >= rs_rel[:, :, None]) & (j[None, None, :] < rs_rel[:, :, None] + NA_KH)
    dy = jnp.clip(j[None, None, :] - q_rel[:, :, None] + NA_KH - 1, 0, 2 * NA_KH - 2)
    cs = jnp.clip(c - NA_KW // 2, 0, GRID_W - NA_KW)
    vcol = (c[None, :] >= cs[:, None]) & (c[None, :] < cs[:, None] + NA_KW)
    dx = jnp.clip(c[None, :] - c[:, None] + NA_KW - 1, 0, 2 * NA_KW - 2)
    b = rpb.astype(F32)[:, dy[:, :, None, :, None], dx[None, None, :, None, :]]
    valid = vrow[:, :, None, :, None] & vcol[None, None, :, None, :]
    b = jnp.where(valid[None], b, NEG)
    return b.transpose(1, 0, 2, 3, 4, 5).reshape(3, H, NA_QTOK, NA_WTOK)


def _na_attn(z, bias, *, groups, heads, k_col, v_col):
    T = z.shape[0]
    (bp, lp), (bs, ls) = groups
    n_p = bp * lp // NA_QTOK
    nbp = lp // NA_QTOK
    nbs = ls // NA_QTOK
    assert nbp >= 2 and nbs >= 2, "needs at least 16 grid rows per sequence"
    piece_per_blk = NA_QTOK // NA_PIECE

    def meta(g):
        is_p = g < n_p
        gl = jnp.where(is_p, g, g - n_p)
        nb = jnp.where(is_p, nbp, nbs)
        seq = gl // nb
        i = gl % nb
        base = jnp.where(is_p, 0, bp * lp // NA_PIECE) + seq * (piece_per_blk * nb)
        w0 = jnp.clip(piece_per_blk * i - 1, 0, piece_per_blk * nb - 4)
        kind = jnp.where(i == 0, 0, jnp.where(i == nb - 1, 2, 1))
        return base + w0, kind

    def piece_spec(col, jj):
        return pl.BlockSpec((NA_PIECE, NA_DH), lambda h, g: (meta(g)[0] + jj, col + h))

    return pl.pallas_call(
        _na_kernel,
        grid=(heads, T // NA_QTOK),
        in_specs=([pl.BlockSpec((NA_QTOK, NA_DH), lambda h, g: (g, h))]
                  + [piece_spec(k_col // NA_DH, jj) for jj in range(4)]
                  + [piece_spec(v_col // NA_DH, jj) for jj in range(4)]
                  + [pl.BlockSpec((None, None, NA_QTOK, NA_WTOK), lambda h, g: (meta(g)[1], h, 0, 0))]),
        out_specs=pl.BlockSpec((NA_QTOK, NA_DH), lambda h, g: (g, h)),
        out_shape=jax.ShapeDtypeStruct((T, heads * NA_DH), BF16),
        compiler_params=_cparams("parallel", "parallel"),
        name="na_attn",
    )(z, *([z] * 8), bias)


def _merge_kernel(ya_ref, yb_ref, ga_ref, gb_ref, x_ref, wa_ref, wb_ref, wo_ref, gm_ref, wr_ref, br_ref,
                  x1_ref, h2_ref, ridx_ref, rgate_ref):
    ya = jnp.dot(ya_ref[...], wa_ref[...], preferred_element_type=F32)
    yb = jnp.dot(yb_ref[...], wb_ref[...], preferred_element_type=F32)
    u = _sigmoid(ga_ref[...].astype(F32)) * ya + _sigmoid(gb_ref[...].astype(F32)) * yb
    x1 = x_ref[...] + jnp.dot(u.astype(BF16), wo_ref[...], preferred_element_type=F32)
    x1_ref[...] = x1
    h2 = _rms(x1, gm_ref[...])
    for s in range(h2.shape[1] // LANE):
        h2_ref[:, s, :] = h2[:, s * LANE:(s + 1) * LANE]
    logits = jnp.dot(h2.astype(BF16), wr_ref[...], preferred_element_type=F32) + br_ref[...]
    lane = lax.broadcasted_iota(I32, logits.shape, 1)
    vals, idxs = [], []
    for _ in range(TOP_K):
        mx = jnp.max(logits, axis=-1, keepdims=True)
        ix = jnp.min(jnp.where(logits == mx, lane, LANE), axis=-1, keepdims=True)
        vals.append(mx)
        idxs.append(ix)
        logits = jnp.where(lane == ix, 2 * NEG, logits)
    es = [jnp.exp(v - vals[0]) for v in vals]
    den = functools.reduce(lambda a, b: a + b, es)
    ridx = jnp.zeros(lane.shape, I32)
    rgate = jnp.zeros(lane.shape, F32)
    for k in range(TOP_K):
        ridx = jnp.where(lane == k, idxs[k], ridx)
        rgate = jnp.where(lane == k, es[k] / den, rgate)
    ridx_ref[...] = ridx
    rgate_ref[...] = rgate


def _merge(ya, yb, z, x, wa, wb, wo, gm, wr, br, *, ga_col):
    T, D = x.shape
    tm = 256
    S = D // LANE
    return pl.pallas_call(
        _merge_kernel,
        grid=(T // tm,),
        in_specs=[pl.BlockSpec((tm, D), lambda i: (i, 0)),
                  pl.BlockSpec((tm, D), lambda i: (i, 0)),
                  pl.BlockSpec((tm, D), lambda i: (i, ga_col // D)),
                  pl.BlockSpec((tm, D), lambda i: (i, ga_col // D + 1)),
                  pl.BlockSpec((tm, D), lambda i: (i, 0)),
                  _resident(wa.shape), _resident(wb.shape), _resident(wo.shape),
                  _resident((1, D)), _resident((D, LANE)), _resident((1, LANE))],
        out_specs=[pl.BlockSpec((tm, D), lambda i: (i, 0)),
                   pl.BlockSpec((tm, S, LANE), lambda i: (i, 0, 0)),
                   pl.BlockSpec((tm, LANE), lambda i: (i, 0)),
                   pl.BlockSpec((tm, LANE), lambda i: (i, 0))],
        out_shape=[jax.ShapeDtypeStruct((T, D), F32), jax.ShapeDtypeStruct((T, S, LANE), F32),
                   jax.ShapeDtypeStruct((T, LANE), I32), jax.ShapeDtypeStruct((T, LANE), F32)],
        compiler_params=_cparams("parallel"),
        name="merge_router",
    )(ya, yb, z, z, x, wa, wb, wo, gm, wr, br)


def _dispatch_kernel(dest_ref, h_ref, xs_in_ref, xs_ref, sem, *, tm):
    del xs_in_ref

    def row_copy(r, d):
        return pltpu.make_async_copy(h_ref.at[r], xs_ref.at[d], sem)

    def body(r, c):
        for k in range(TOP_K):
            row_copy(r, dest_ref[TOP_K * r + k]).start()
        return c

    lax.fori_loop(0, tm, body, 0)
    for _ in range(TOP_K):
        pltpu.make_async_copy(h_ref, xs_ref.at[pl.ds(0, tm)], sem).wait()


def _dispatch(dest, h3, n_rows):
    T, S, _ = h3.shape
    tm = 256
    xs0 = jnp.zeros((n_rows, S, LANE), F32)
    return pl.pallas_call(
        functools.partial(_dispatch_kernel, tm=tm),
        grid=(T // tm,),
        in_specs=[pl.BlockSpec((TOP_K * tm,), lambda i: (i,), memory_space=pltpu.SMEM),
                  pl.BlockSpec((tm, S, LANE), lambda i: (i, 0, 0)),
                  pl.BlockSpec(memory_space=pl.ANY)],
        out_specs=pl.BlockSpec(memory_space=pl.ANY),
        out_shape=jax.ShapeDtypeStruct((n_rows, S, LANE), F32),
        scratch_shapes=[pltpu.SemaphoreType.DMA],
        input_output_aliases={2: 0},
        compiler_params=_cparams("arbitrary"),
        name="moe_dispatch",
    )(dest, h3, xs0)


def _expert_kernel(be_ref, nu_ref, xs_ref, wg_ref, bg_ref, wu_ref, bu_ref, wd_ref, bd_ref, ys_ref,
                   xb_scr, acc_scr, *, nf):
    i = pl.program_id(0)
    f = pl.program_id(1)
    S = xs_ref.shape[1]
    used = i < nu_ref[0]

    @pl.when(used)
    def _():
        @pl.when(f == 0)
        def _():
            for s in range(S):
                xb_scr[:, s * LANE:(s + 1) * LANE] = xs_ref[:, s, :].astype(BF16)

        xb = xb_scr[...]
        g = jnp.dot(xb, wg_ref[0], preferred_element_type=F32) + bg_ref[0]
        u = jnp.dot(xb, wu_ref[0], preferred_element_type=F32) + bu_ref[0]
        g = jnp.minimum(g, SWIGLU_LIMIT)
        u = jnp.clip(u, -SWIGLU_LIMIT, SWIGLU_LIMIT)
        a = g * _sigmoid(SWIGLU_ALPHA * g) * (u + 1.0)
        part = jnp.dot(a.astype(BF16), wd_ref[0], preferred_element_type=F32)

        @pl.when(f == 0)
        def _():
            acc_scr[...] = part + bd_ref[0]

        @pl.when(f > 0)
        def _():
            acc_scr[...] += part

    @pl.when(f == nf - 1)
    def _():
        @pl.when(used)
        def _():
            for s in range(S):
                ys_ref[:, s, :] = acc_scr[:, s * LANE:(s + 1) * LANE]

        @pl.when(jnp.logical_not(used))
        def _():
            ys_ref[...] = jnp.zeros(ys_ref.shape, F32)


def _experts(block_e, n_used, xs, wg, bg, wu, bu, wd, bd):
    P, S, _ = xs.shape
    E, D, F = wg.shape
    tf = _pick(F, (512, 256, 128))
    nf = F // tf
    nblk = P // MOE_BM

    def blk(i, nu):
        return jnp.minimum(i, nu[0] - 1)

    def fidx(i, f, nu):
        return jnp.where(i < nu[0], f, nf - 1)

    grid_spec = pltpu.PrefetchScalarGridSpec(
        num_scalar_prefetch=2,
        grid=(nblk, nf),
        in_specs=[pl.BlockSpec((MOE_BM, S, LANE), lambda i, f, be, nu: (blk(i, nu), 0, 0)),
                  pl.BlockSpec((1, D, tf), lambda i, f, be, nu: (be[blk(i, nu)], 0, fidx(i, f, nu))),
                  pl.BlockSpec((1, 1, tf), lambda i, f, be, nu: (be[blk(i, nu)], 0, fidx(i, f, nu))),
                  pl.BlockSpec((1, D, tf), lambda i, f, be, nu: (be[blk(i, nu)], 0, fidx(i, f, nu))),
                  pl.BlockSpec((1, 1, tf), lambda i, f, be, nu: (be[blk(i, nu)], 0, fidx(i, f, nu))),
                  pl.BlockSpec((1, tf, D), lambda i, f, be, nu: (be[blk(i, nu)], fidx(i, f, nu), 0)),
                  pl.BlockSpec((1, 1, D), lambda i, f, be, nu: (be[blk(i, nu)], 0, 0))],
        out_specs=pl.BlockSpec((MOE_BM, S, LANE), lambda i, f, be, nu: (i, 0, 0)),
        scratch_shapes=[pltpu.VMEM((MOE_BM, D), BF16), pltpu.VMEM((MOE_BM, D), F32)],
    )
    return pl.pallas_call(
        functools.partial(_expert_kernel, nf=nf),
        grid_spec=grid_spec,
        out_shape=jax.ShapeDtypeStruct((P, S, LANE), F32),
        compiler_params=_cparams("arbitrary", "arbitrary"),
        name="moe_experts",
    )(block_e, n_used, xs, wg, bg, wu, bu, wd, bd)


def _combine_kernel(pos_ref, ys_ref, x1_ref, gate_ref, p_ref, gp_ref, wpg_ref, wpp_ref, gf_ref, o_ref,
                    buf, x2_scr, sem, *, tm, final):
    def row_copy(d, k, r):
        return pltpu.make_async_copy(ys_ref.at[d], buf.at[k, r], sem)

    def body(r, c):
        for k in range(TOP_K):
            row_copy(pos_ref[TOP_K * r + k], k, r).start()
        return c

    lax.fori_loop(0, tm, body, 0)
    for k in range(TOP_K):
        pltpu.make_async_copy(ys_ref.at[pl.ds(0, tm)], buf.at[k], sem).wait()

    gates = gate_ref[...]
    S = buf.shape[2]
    for s in range(S):
        moe = gates[:, 0:1] * buf[0, :, s, :]
        for k in range(1, TOP_K):
            moe = moe + gates[:, k:k + 1] * buf[k, :, s, :]
        x2_scr[:, s * LANE:(s + 1) * LANE] = x1_ref[:, s * LANE:(s + 1) * LANE] + moe
    x2 = x2_scr[...]
    h3 = _rms(x2, gp_ref[...]).astype(BF16)
    gt = _sigmoid(jnp.dot(h3, wpg_ref[...], preferred_element_type=F32))
    pe = jnp.dot(p_ref[...].astype(BF16), wpp_ref[...], preferred_element_type=F32)
    x3 = x2 + gt * pe
    o_ref[...] = _rms(x3, gf_ref[...]) if final else x3


def _combine(pos, ys, x1, gates, p, gp, wpg, wpp, gf, *, tok_off, n_tok, final):
    D = x1.shape[1]
    S = D // LANE
    tm = 256
    off = tok_off // tm
    return pl.pallas_call(
        functools.partial(_combine_kernel, tm=tm, final=final),
        grid=(n_tok // tm,),
        in_specs=[pl.BlockSpec((TOP_K * tm,), lambda i: (off + i,), memory_space=pltpu.SMEM),
                  pl.BlockSpec(memory_space=pl.ANY),
                  pl.BlockSpec((tm, D), lambda i: (off + i, 0)),
                  pl.BlockSpec((tm, LANE), lambda i: (off + i, 0)),
                  pl.BlockSpec((tm, p.shape[1]), lambda i: (off + i, 0)),
                  _resident((1, D)), _resident(wpg.shape), _resident(wpp.shape), _resident((1, D))],
        out_specs=pl.BlockSpec((tm, D), lambda i: (i, 0)),
        out_shape=jax.ShapeDtypeStruct((n_tok, D), F32),
        scratch_shapes=[pltpu.VMEM((TOP_K, tm, S, LANE), F32), pltpu.VMEM((tm, D), F32),
                        pltpu.SemaphoreType.DMA],
        compiler_params=_cparams("arbitrary"),
        name="moe_combine_ple",
    )(pos, ys, x1, gates, p, gp, wpg, wpp, gf)


def _route(ridx, n_exp):
    flat_e = ridx.reshape(-1)
    tk = flat_e.shape[0]
    oh = (flat_e[:, None] == jnp.arange(n_exp, dtype=I32)[None, :]).astype(I32)
    csum = jnp.cumsum(oh, axis=0)
    rank = jnp.sum(oh * csum, axis=1) - 1
    counts = csum[-1]
    pcounts = (counts + MOE_BM - 1) // MOE_BM * MOE_BM
    pend = jnp.cumsum(pcounts)
    pstart = pend - pcounts
    dest = (pstart[flat_e] + rank).astype(I32)
    nblk = (tk + n_exp * (MOE_BM - 1) + MOE_BM - 1) // MOE_BM
    block_e = jnp.minimum(
        jnp.searchsorted(pend, jnp.arange(nblk, dtype=I32) * MOE_BM, side="right"), n_exp - 1).astype(I32)
    n_used = (pend[-1:] // MOE_BM).astype(I32)
    return dest, block_e, n_used, nblk * MOE_BM


def _swap_halves(w):
    half = w.shape[-1] // 2
    return jnp.concatenate([w[..., half:], w[..., :half]], axis=-1)


def _rope_tables(length):
    pos = jnp.arange(length, dtype=F32)
    inv_freq = ROPE_THETA ** (-jnp.arange(0, ROPE, 2, dtype=F32) / ROPE)
    ang = pos[:, None] * inv_freq[None, :]
    cos, sin = jnp.cos(ang), jnp.sin(ang)
    zeros = jnp.zeros((length, LANE - ROPE), F32)
    return (jnp.concatenate([cos, cos, zeros], axis=-1), jnp.concatenate([-sin, sin, zeros], axis=-1))


def kernel(x_prompt, x_sample, p_prompt, p_sample, g_mix, w_in, g_q_lat, w_q_up, g_kv_lat, w_kv_up, na_rpb, w_br_a, w_br_b, w_out, g_moe, w_router, b_router, w_gate, b_gate, w_up, b_up, w_down, b_down, g_ple, w_ple_gate, w_ple_proj, g_final):
    bp, lp, D = x_prompt.shape
    bs, ls, _ = x_sample.shape
    tp, ts = bp * lp, bs * ls
    groups = ((bp, lp), (bs, ls))
    depth = w_in.shape[0]
    na_heads = na_rpb.shape[1]
    na_w = na_heads * NA_DH
    ql = g_q_lat.shape[-1]
    kvl = g_kv_lat.shape[-1]
    mla_heads = w_q_up.shape[-1] // (NOPE + ROPE)
    n_exp = w_router.shape[-1]
    assert na_w == D and ql == kvl and lp >= ls

    x = jnp.concatenate([x_prompt.reshape(tp, D), x_sample.reshape(ts, D)], axis=0)
    cos_t, sin_t = _rope_tables(lp)
    outs = None
    for li in range(depth):
        p = jnp.concatenate([p_prompt[li].reshape(tp, -1), p_sample[li].reshape(ts, -1)], axis=0)
        wi = w_in[li]
        o = [0]
        for wdt in (na_w, na_w, na_w, ql, kvl, ROPE, D, D):
            o.append(o[-1] + wdt)
        w_qa, w_ka, w_va, w_cq, w_ckv, w_kr, w_ga, w_gb = (wi[:, o[k]:o[k + 1]] for k in range(8))
        w_main = jnp.concatenate([w_qa * (NA_DH ** -0.5), w_ka, w_va, w_ga, w_gb, w_cq, w_ckv],
                                 axis=1).astype(BF16)
        ga_col = 3 * na_w
        cq_col = 3 * na_w + 2 * D
        w_kr2 = jnp.concatenate([w_kr, _swap_halves(w_kr)], axis=1).astype(BF16)
        wq3 = w_q_up[li].reshape(ql, mla_heads, NOPE + ROPE) * ((NOPE + ROPE) ** -0.5)
        wq = jnp.concatenate([wq3[..., :NOPE], wq3[..., NOPE:], _swap_halves(wq3[..., NOPE:])],
                             axis=-1).reshape(ql, mla_heads * HEAD_PAD).astype(BF16)
        wkv = w_kv_up[li].astype(BF16)
        bias = _na_bias(na_rpb[li])
        wr = jnp.pad(w_router[li], ((0, 0), (0, LANE - n_exp))).astype(BF16)
        br = jnp.pad(b_router[li].astype(F32), (0, LANE - n_exp), constant_values=NEG)[None, :]

        z, kr = _inproj(x, g_mix[li][None, :], w_main, w_kr2)
        qp, kp, vp = _mlaproj(z, kr, cos_t, sin_t, g_q_lat[li][None, :], g_kv_lat[li][None, :], wq, wkv,
                              cq_col=cq_col, groups=groups, heads=mla_heads)
        ya = _na_attn(z, bias, groups=groups, heads=na_heads, k_col=na_w, v_col=2 * na_w)
        yb = jnp.concatenate([
            _mla_attn(qp, kp, vp, batch=bp, length=lp, tok_off=0, heads=mla_heads),
            _mla_attn(qp, kp, vp, batch=bs, length=ls, tok_off=tp, heads=mla_heads)], axis=0)
        x1, h2, ridx, rgate = _merge(ya, yb, z, x, w_br_a[li].astype(BF16), w_br_b[li].astype(BF16),
                                     w_out[li].astype(BF16), g_moe[li][None, :], wr, br, ga_col=ga_col)

        dest, block_e, n_used, n_rows = _route(ridx[:, :TOP_K], n_exp)
        xs = _dispatch(dest, h2, n_rows)
        ys = _experts(block_e, n_used, xs,
                      w_gate[li].astype(BF16), b_gate[li][:, None, :], w_up[li].astype(BF16),
                      b_up[li][:, None, :], w_down[li].astype(BF16), b_down[li][:, None, :])
        final = li == depth - 1
        comb = functools.partial(_combine, dest, ys, x1, rgate, p, g_ple[li][None, :],
                                 w_ple_gate[li].astype(BF16), w_ple_proj[li].astype(BF16),
                                 g_final[None, :], final=final)
        outs = (comb(tok_off=0, n_tok=tp), comb(tok_off=tp, n_tok=ts))
        if not final:
            x = jnp.concatenate(outs, axis=0)
    return (outs[0].reshape(bp, lp, D), outs[1].reshape(bs, ls, D))
```

```python
import functools

import jax
import jax.numpy as jnp
from jax import lax
from jax.experimental import pallas as pl
from jax.experimental.pallas import tpu as pltpu

F32 = jnp.float32
BF16 = jnp.bfloat16
I32 = jnp.int32
U32 = jnp.uint32

LANE = 128
SUBLANE = 8
ROW_WORDS = SUBLANE * LANE
VMEM_LIMIT_BYTES = 56 * 1024 * 1024

RMS_EPS = 1e-6
GRID_W = 64
NA_KH = 8
NA_KW = 16
NA_DH = 128
NA_QROWS = 8
NA_QTOK = NA_QROWS * GRID_W
NA_WTOK = 2 * NA_QTOK
NA_PIECE = NA_WTOK // 4
NOPE = 128
ROPE = 64
V_DIM = 128
HEAD_PAD = 256
ROPE_THETA = 10000.0
TOP_K = 4
SWIGLU_LIMIT = 7.0
SWIGLU_ALPHA = 1.702
MOE_BM = 512
MLA_TQ_CANDS = (1024, 512)
MLA_TK_CANDS = (1024, 512, 256)
LOG2E = 1.4426950408889634
NEG = -1e30


def _cparams(*sem):
    return pltpu.CompilerParams(dimension_semantics=sem, vmem_limit_bytes=VMEM_LIMIT_BYTES)


def _pick(n, cands):
    for c in cands:
        if n % c == 0:
            return c
    raise ValueError(f"no tile in {cands} divides {n}")


def _resident(shape):
    nd = len(shape)
    return pl.BlockSpec(shape, lambda *_: (0,) * nd, pipeline_mode=pl.Buffered(1))


def _rms(x, g):
    return x * lax.rsqrt(jnp.mean(x * x, axis=-1, keepdims=True) + RMS_EPS) * g


def _sigmoid(x):
    return 1.0 / (1.0 + jnp.exp(-x))


def _pack_pair(lo, hi):
    lo_bits = lax.bitcast_convert_type(lo.astype(BF16).astype(F32), U32) >> 16
    hi_bits = lax.bitcast_convert_type(hi.astype(BF16).astype(F32), U32)
    return lo_bits | hi_bits


def _unpack_pair(w):
    lo = lax.bitcast_convert_type(w << 16, F32)
    hi = lax.bitcast_convert_type(w & jnp.uint32(0xFFFF0000), F32)
    return lo, hi


def _store_packed_rows(ref, x):
    rows, d = x.shape
    assert d == 2 * ROW_WORDS
    for i in range(SUBLANE):
        lo = x[:, i * LANE:(i + 1) * LANE]
        hi = x[:, d // 2 + i * LANE:d // 2 + (i + 1) * LANE]
        ref[pl.ds(i, rows, stride=SUBLANE), :] = _pack_pair(lo, hi)


def _load_packed_rows(ref, rows):
    half = ROW_WORDS
    for i in range(SUBLANE):
        lo, hi = _unpack_pair(ref[pl.ds(i, rows, stride=SUBLANE), :])
        yield i * LANE, lo
        yield half + i * LANE, hi


def _inproj_kernel(x_ref, g_ref, w_ref, wkr_ref, z_ref, kr_ref, h_scr):
    @pl.when(pl.program_id(1) == 0)
    def _():
        hb = _rms(x_ref[...], g_ref[...]).astype(BF16)
        h_scr[...] = hb
        kr_ref[...] = jnp.dot(hb, wkr_ref[...], preferred_element_type=F32)

    z_ref[...] = jnp.dot(h_scr[...], w_ref[...], preferred_element_type=F32).astype(z_ref.dtype)


def _inproj(x, g, w, wkr):
    T, D = x.shape
    N = w.shape[1]
    tm = _pick(T, (1024, 512, 256))
    tn = _pick(N, (1024, 512, 256, 128))
    return pl.pallas_call(
        _inproj_kernel,
        grid=(T // tm, N // tn),
        in_specs=[pl.BlockSpec((tm, D), lambda i, j: (i, 0)),
                  pl.BlockSpec((1, D), lambda i, j: (0, 0)),
                  pl.BlockSpec((D, tn), lambda i, j: (0, j)),
                  pl.BlockSpec((D, LANE), lambda i, j: (0, 0))],
        out_specs=[pl.BlockSpec((tm, tn), lambda i, j: (i, j)),
                   pl.BlockSpec((tm, LANE), lambda i, j: (i, 0))],
        out_shape=[jax.ShapeDtypeStruct((T, N), BF16), jax.ShapeDtypeStruct((T, LANE), F32)],
        scratch_shapes=[pltpu.VMEM((tm, D), BF16)],
        compiler_params=_cparams("parallel", "arbitrary"),
        name="inproj",
    )(x, g, w, wkr)


def _mlaproj_kernel(cq_ref, ckv_ref, kr_ref, cos_ref, sin_ref, gq_ref, gkv_ref, wq_ref, wkv_ref,
                    q_out, k_out, v_out, *, heads):
    cos = cos_ref[...]
    sin = sin_ref[...]

    def rope(b):
        return b * cos + pltpu.roll(b, ROPE, axis=1) * sin

    q = jnp.dot(_rms(cq_ref[...].astype(F32), gq_ref[...]).astype(BF16), wq_ref[...],
                preferred_element_type=F32)
    kv = jnp.dot(_rms(ckv_ref[...].astype(F32), gkv_ref[...]).astype(BF16), wkv_ref[...],
                 preferred_element_type=F32)
    krope = rope(kr_ref[...]).astype(BF16)
    ones = jnp.ones(krope.shape, BF16)
    for h in range(heads):
        c = h * HEAD_PAD
        q_out[:, c:c + NOPE] = q[:, c:c + NOPE].astype(BF16)
        q_out[:, c + NOPE:c + HEAD_PAD] = rope(q[:, c + NOPE:c + HEAD_PAD]).astype(BF16)
        k_out[:, c:c + NOPE] = kv[:, c:c + NOPE].astype(BF16)
        k_out[:, c + NOPE:c + HEAD_PAD] = krope
        v_out[:, c:c + V_DIM] = kv[:, c + NOPE:c + HEAD_PAD].astype(BF16)
        v_out[:, c + V_DIM:c + HEAD_PAD] = ones


def _mlaproj(z, kr, cos_t, sin_t, gq, gkv, wq, wkv, *, cq_col, groups, heads):
    T = z.shape[0]
    ql = gq.shape[-1]
    kvl = gkv.shape[-1]
    W = heads * HEAD_PAD
    (bp, lp), (bs, ls) = groups
    tm = _pick(ls, (256, 128))
    n_p = bp * lp // tm

    def pos_map(i):
        return (jnp.where(i < n_p, i % (lp // tm), (i - n_p) % (ls // tm)), 0)

    out = jax.ShapeDtypeStruct((T, W), BF16)
    return pl.pallas_call(
        functools.partial(_mlaproj_kernel, heads=heads),
        grid=(T // tm,),
        in_specs=[pl.BlockSpec((tm, ql), lambda i: (i, cq_col // ql)),
                  pl.BlockSpec((tm, kvl), lambda i: (i, (cq_col + ql) // kvl)),
                  pl.BlockSpec((tm, LANE), lambda i: (i, 0)),
                  pl.BlockSpec((tm, LANE), pos_map),
                  pl.BlockSpec((tm, LANE), pos_map),
                  _resident((1, ql)), _resident((1, kvl)),
                  _resident((ql, W)), _resident((kvl, W))],
        out_specs=[pl.BlockSpec((tm, W), lambda i: (i, 0))] * 3,
        out_shape=[out, out, out],
        compiler_params=_cparams("parallel"),
        name="mlaproj",
    )(z, z, kr, cos_t, sin_t, gq, gkv, wq, wkv)


def _mla_attn_kernel(q_ref, k_ref, v_ref, o_ref, s0, s1, p0, p1, a0, a1, m_scr, acc_scr, *, tk, nk):
    q = q_ref[...]

    def qk(c, s_ref):
        off = pl.multiple_of(c * tk, tk)
        s_ref[...] = lax.dot_general(q, k_ref[pl.ds(off, tk), :], (((1,), (1,)), ((), ())),
                                     preferred_element_type=F32)

    def sm(s_ref, p_ref, a_ref):
        s = s_ref[...]
        m_prev = m_scr[...]
        m_new = jnp.maximum(m_prev, jnp.max(s, axis=-1, keepdims=True))
        a_ref[...] = jnp.exp2(m_prev - m_new)
        p_ref[...] = jnp.exp2(s - m_new).astype(BF16)
        m_scr[...] = m_new

    def pv(c, p_ref, a_ref):
        off = pl.multiple_of(c * tk, tk)
        acc_scr[...] = acc_scr[...] * a_ref[...] + jnp.dot(p_ref[...], v_ref[pl.ds(off, tk), :],
                                                           preferred_element_type=F32)

    m_scr[...] = jnp.full(m_scr.shape, NEG, F32)
    acc_scr[...] = jnp.zeros(acc_scr.shape, F32)
    qk(0, s0)
    qk(1, s1)
    sm(s0, p0, a0)

    def body(j, carry):
        qk(2 * j, s0)
        sm(s1, p1, a1)
        pv(2 * j - 2, p0, a0)
        qk(2 * j + 1, s1)
        sm(s0, p0, a0)
        pv(2 * j - 1, p1, a1)
        return carry

    lax.fori_loop(1, nk // 2, body, 0)
    sm(s1, p1, a1)
    pv(nk - 2, p0, a0)
    pv(nk - 1, p1, a1)
    acc = acc_scr[...]
    o_ref[...] = (acc[:, :V_DIM] / acc[:, V_DIM:V_DIM + 1]).astype(o_ref.dtype)


def _mla_attn(qp, kp, vp, *, batch, length, tok_off, heads):
    tq = _pick(length, MLA_TQ_CANDS)
    tk = _pick(length // 2, MLA_TK_CANDS)
    nk = length // tk
    assert nk % 2 == 0 and tok_off % length == 0
    nq = length // tq
    qoff = tok_off // tq
    boff = tok_off // length
    return pl.pallas_call(
        functools.partial(_mla_attn_kernel, tk=tk, nk=nk),
        grid=(batch, heads, nq),
        in_specs=[pl.BlockSpec((tq, HEAD_PAD), lambda b, h, i: (qoff + b * nq + i, h)),
                  pl.BlockSpec((length, HEAD_PAD), lambda b, h, i: (boff + b, h)),
                  pl.BlockSpec((length, HEAD_PAD), lambda b, h, i: (boff + b, h))],
        out_specs=pl.BlockSpec((tq, V_DIM), lambda b, h, i: (b * nq + i, h)),
        out_shape=jax.ShapeDtypeStruct((batch * length, heads * V_DIM), BF16),
        scratch_shapes=[pltpu.VMEM((tq, tk), F32), pltpu.VMEM((tq, tk), F32),
                        pltpu.VMEM((tq, tk), BF16), pltpu.VMEM((tq, tk), BF16),
                        pltpu.VMEM((tq, 1), F32), pltpu.VMEM((tq, 1), F32),
                        pltpu.VMEM((tq, 1), F32), pltpu.VMEM((tq, HEAD_PAD), F32)],
        compiler_params=_cparams("parallel", "parallel", "arbitrary"),
        name="mla_attn",
    )(qp, kp, vp)


def _na_kernel(q_ref, k0, k1, k2, k3, v0, v1, v2, v3, b_ref, o_ref):
    q = q_ref[...]
    pw = NA_PIECE
    ss = []
    for j, kr in enumerate((k0, k1, k2, k3)):
        s = lax.dot_general(q, kr[...], (((1,), (1,)), ((), ())), preferred_element_type=F32)
        ss.append(s + b_ref[:, j * pw:(j + 1) * pw])
    m = functools.reduce(jnp.maximum, [jnp.max(s, axis=-1, keepdims=True) for s in ss])
    acc = None
    den = None
    for s, vr in zip(ss, (v0, v1, v2, v3)):
        p = jnp.exp(s - m)
        d = jnp.sum(p, axis=-1, keepdims=True)
        a = jnp.dot(p.astype(BF16), vr[...], preferred_element_type=F32)
        acc = a if acc is None else acc + a
        den = d if den is None else den + d
    o_ref[...] = (acc / den).astype(o_ref.dtype)


def _na_bias(rpb):
    H = rpb.shape[0]
    a = jnp.arange(NA_QROWS)
    j = jnp.arange(2 * NA_QROWS)
    c = jnp.arange(GRID_W)
    q_rel = jnp.stack([a, a + NA_KH // 2, a + NA_QROWS])
    rs_rel = jnp.stack([jnp.maximum(a - NA_KH // 2, 0), a, jnp.minimum(a + NA_KH // 2, NA_QROWS)])
    vrow = (j[None, None, :] >= rs_rel[:, :, None]) & (j[None, None, :] < rs_rel[:, :, None] + NA_KH)
    dy = jnp.clip(j[None, None, :] - q_rel[:, :, None] + NA_KH - 1, 0, 2 * NA_KH - 2)
    cs = jnp.clip(c - NA_KW // 2, 0, GRID_W - NA_KW)
    vcol = (c[None, :] >= cs[:, None]) & (c[None, :] < cs[:, None] + NA_KW)
    dx = jnp.clip(c[None, :] - c[:, None] + NA_KW - 1, 0, 2 * NA_KW - 2)
    oh_x = (dx[:, :, None] == jnp.arange(2 * NA_KW - 1)[None, None, :]).astype(F32)
    oh_y = (dy[..., None] == jnp.arange(2 * NA_KH - 1)[None, None, None, :]).astype(F32)
    b2 = jnp.einsum("hyx,cdx->hycd", rpb.astype(F32), oh_x, precision=lax.Precision.HIGHEST)
    b = jnp.einsum("kajy,hycd->hkacjd", oh_y, b2, precision=lax.Precision.HIGHEST)
    valid = vrow[:, :, None, :, None] & vcol[None, None, :, None, :]
    b = jnp.where(valid[None], b, NEG)
    return b.transpose(1, 0, 2, 3, 4, 5).reshape(3, H, NA_QTOK, NA_WTOK)


def _na_attn(z, bias, *, groups, heads, k_col, v_col):
    T = z.shape[0]
    (bp, lp), (bs, ls) = groups
    n_p = bp * lp // NA_QTOK
    nbp = lp // NA_QTOK
    nbs = ls // NA_QTOK
    assert nbp >= 2 and nbs >= 2, "needs at least 16 grid rows per sequence"
    piece_per_blk = NA_QTOK // NA_PIECE

    def meta(g):
        is_p = g < n_p
        gl = jnp.where(is_p, g, g - n_p)
        nb = jnp.where(is_p, nbp, nbs)
        seq = gl // nb
        i = gl % nb
        base = jnp.where(is_p, 0, bp * lp // NA_PIECE) + seq * (piece_per_blk * nb)
        w0 = jnp.clip(piece_per_blk * i - 1, 0, piece_per_blk * nb - 4)
        kind = jnp.where(i == 0, 0, jnp.where(i == nb - 1, 2, 1))
        return base + w0, kind

    def piece_spec(col, jj):
        return pl.BlockSpec((NA_PIECE, NA_DH), lambda h, g: (meta(g)[0] + jj, col + h))

    return pl.pallas_call(
        _na_kernel,
        grid=(heads, T // NA_QTOK),
        in_specs=([pl.BlockSpec((NA_QTOK, NA_DH), lambda h, g: (g, h))]
                  + [piece_spec(k_col // NA_DH, jj) for jj in range(4)]
                  + [piece_spec(v_col // NA_DH, jj) for jj in range(4)]
                  + [pl.BlockSpec((None, None, NA_QTOK, NA_WTOK), lambda h, g: (meta(g)[1], h, 0, 0))]),
        out_specs=pl.BlockSpec((NA_QTOK, NA_DH), lambda h, g: (g, h)),
        out_shape=jax.ShapeDtypeStruct((T, heads * NA_DH), BF16),
        compiler_params=_cparams("parallel", "parallel"),
        name="na_attn",
    )(z, *([z] * 8), bias)


def _merge_kernel(ya_ref, yb_ref, ga_ref, gb_ref, x_ref, wa_ref, wb_ref, wo_ref, gm_ref, wr_ref, br_ref,
                  x1_ref, h2_ref, ridx_ref, rgate_ref):
    ya = jnp.dot(ya_ref[...], wa_ref[...], preferred_element_type=F32)
    yb = jnp.dot(yb_ref[...], wb_ref[...], preferred_element_type=F32)
    u = _sigmoid(ga_ref[...].astype(F32)) * ya + _sigmoid(gb_ref[...].astype(F32)) * yb
    x1 = x_ref[...] + jnp.dot(u.astype(BF16), wo_ref[...], preferred_element_type=F32)
    x1_ref[...] = x1
    h2 = _rms(x1, gm_ref[...])
    _store_packed_rows(h2_ref, h2)
    logits = jnp.dot(h2.astype(BF16), wr_ref[...], preferred_element_type=F32) + br_ref[...]
    lane = lax.broadcasted_iota(I32, logits.shape, 1)
    vals, idxs = [], []
    for _ in range(TOP_K):
        mx = jnp.max(logits, axis=-1, keepdims=True)
        ix = jnp.min(jnp.where(logits == mx, lane, LANE), axis=-1, keepdims=True)
        vals.append(mx)
        idxs.append(ix)
        logits = jnp.where(lane == ix, 2 * NEG, logits)
    es = [jnp.exp(v - vals[0]) for v in vals]
    den = functools.reduce(lambda a, b: a + b, es)
    ridx = jnp.zeros(lane.shape, I32)
    rgate = jnp.zeros(lane.shape, F32)
    for k in range(TOP_K):
        ridx = jnp.where(lane == k, idxs[k], ridx)
        rgate = jnp.where(lane == k, es[k] / den, rgate)
    ridx_ref[...] = ridx
    rgate_ref[...] = rgate


def _merge(ya, yb, z, x, wa, wb, wo, gm, wr, br, *, ga_col):
    T, D = x.shape
    tm = 256
    return pl.pallas_call(
        _merge_kernel,
        grid=(T // tm,),
        in_specs=[pl.BlockSpec((tm, D), lambda i: (i, 0)),
                  pl.BlockSpec((tm, D), lambda i: (i, 0)),
                  pl.BlockSpec((tm, D), lambda i: (i, ga_col // D)),
                  pl.BlockSpec((tm, D), lambda i: (i, ga_col // D + 1)),
                  pl.BlockSpec((tm, D), lambda i: (i, 0)),
                  _resident(wa.shape), _resident(wb.shape), _resident(wo.shape),
                  _resident((1, D)), _resident((D, LANE)), _resident((1, LANE))],
        out_specs=[pl.BlockSpec((tm, D), lambda i: (i, 0)),
                   pl.BlockSpec((tm * SUBLANE, LANE), lambda i: (i, 0)),
                   pl.BlockSpec((tm, LANE), lambda i: (i, 0)),
                   pl.BlockSpec((tm, LANE), lambda i: (i, 0))],
        out_shape=[jax.ShapeDtypeStruct((T, D), F32), jax.ShapeDtypeStruct((T * SUBLANE, LANE), U32),
                   jax.ShapeDtypeStruct((T, LANE), I32), jax.ShapeDtypeStruct((T, LANE), F32)],
        compiler_params=_cparams("parallel"),
        name="merge_router",
    )(ya, yb, z, z, x, wa, wb, wo, gm, wr, br)


def _row_tile(ref, r):
    return ref.at[pl.ds(pl.multiple_of(r * SUBLANE, SUBLANE), SUBLANE), :]


def _dispatch_kernel(dest_ref, h_ref, xs_in_ref, xs_ref, sem, *, tm):
    del xs_in_ref

    def body(r, c):
        for k in range(TOP_K):
            pltpu.make_async_copy(_row_tile(h_ref, r), _row_tile(xs_ref, dest_ref[TOP_K * r + k]), sem).start()
        return c

    lax.fori_loop(0, tm, body, 0)
    for _ in range(TOP_K):
        pltpu.make_async_copy(h_ref, xs_ref.at[pl.ds(0, tm * SUBLANE), :], sem).wait()


def _dispatch(dest, h2p, n_rows):
    T = h2p.shape[0] // SUBLANE
    tm = 256
    xs0 = jnp.zeros((n_rows * SUBLANE, LANE), U32)
    return pl.pallas_call(
        functools.partial(_dispatch_kernel, tm=tm),
        grid=(T // tm,),
        in_specs=[pl.BlockSpec((TOP_K * tm,), lambda i: (i,), memory_space=pltpu.SMEM),
                  pl.BlockSpec((tm * SUBLANE, LANE), lambda i: (i, 0)),
                  pl.BlockSpec(memory_space=pl.ANY)],
        out_specs=pl.BlockSpec(memory_space=pl.ANY),
        out_shape=jax.ShapeDtypeStruct((n_rows * SUBLANE, LANE), U32),
        scratch_shapes=[pltpu.SemaphoreType.DMA],
        input_output_aliases={2: 0},
        compiler_params=_cparams("arbitrary"),
        name="moe_dispatch",
    )(dest, h2p, xs0)


def _expert_kernel(be_ref, nu_ref, xs_ref, wg_ref, bg_ref, wu_ref, bu_ref, wd_ref, bd_ref, ys_ref,
                   xb_scr, acc_scr, *, nf):
    i = pl.program_id(0)
    f = pl.program_id(1)
    bm = xb_scr.shape[0]
    used = i < nu_ref[0]

    @pl.when(used)
    def _():
        @pl.when(f == 0)
        def _():
            for col, blk in _load_packed_rows(xs_ref, bm):
                xb_scr[:, col:col + LANE] = blk.astype(BF16)

        xb = xb_scr[...]
        g = jnp.dot(xb, wg_ref[0], preferred_element_type=F32) + bg_ref[0]
        u = jnp.dot(xb, wu_ref[0], preferred_element_type=F32) + bu_ref[0]
        g = jnp.minimum(g, SWIGLU_LIMIT)
        u = jnp.clip(u, -SWIGLU_LIMIT, SWIGLU_LIMIT)
        a = g * _sigmoid(SWIGLU_ALPHA * g) * (u + 1.0)
        part = jnp.dot(a.astype(BF16), wd_ref[0], preferred_element_type=F32)

        @pl.when(f == 0)
        def _():
            acc_scr[...] = part + bd_ref[0]

        @pl.when(f > 0)
        def _():
            acc_scr[...] += part

    @pl.when(f == nf - 1)
    def _():
        @pl.when(used)
        def _():
            _store_packed_rows(ys_ref, acc_scr[...])

        @pl.when(jnp.logical_not(used))
        def _():
            ys_ref[...] = jnp.zeros(ys_ref.shape, U32)


def _experts(block_e, n_used, xs, wg, bg, wu, bu, wd, bd):
    P = xs.shape[0] // SUBLANE
    E, D, F = wg.shape
    tf = _pick(F, (1024, 512, 256, 128))
    nf = F // tf
    nblk = P // MOE_BM

    def blk(i, nu):
        return jnp.minimum(i, nu[0] - 1)

    def fidx(i, f, nu):
        return jnp.where(i < nu[0], f, nf - 1)

    grid_spec = pltpu.PrefetchScalarGridSpec(
        num_scalar_prefetch=2,
        grid=(nblk, nf),
        in_specs=[pl.BlockSpec((MOE_BM * SUBLANE, LANE), lambda i, f, be, nu: (blk(i, nu), 0)),
                  pl.BlockSpec((1, D, tf), lambda i, f, be, nu: (be[blk(i, nu)], 0, fidx(i, f, nu))),
                  pl.BlockSpec((1, 1, tf), lambda i, f, be, nu: (be[blk(i, nu)], 0, fidx(i, f, nu))),
                  pl.BlockSpec((1, D, tf), lambda i, f, be, nu: (be[blk(i, nu)], 0, fidx(i, f, nu))),
                  pl.BlockSpec((1, 1, tf), lambda i, f, be, nu: (be[blk(i, nu)], 0, fidx(i, f, nu))),
                  pl.BlockSpec((1, tf, D), lambda i, f, be, nu: (be[blk(i, nu)], fidx(i, f, nu), 0)),
                  pl.BlockSpec((1, 1, D), lambda i, f, be, nu: (be[blk(i, nu)], 0, 0))],
        out_specs=pl.BlockSpec((MOE_BM * SUBLANE, LANE), lambda i, f, be, nu: (i, 0)),
        scratch_shapes=[pltpu.VMEM((MOE_BM, D), BF16), pltpu.VMEM((MOE_BM, D), F32)],
    )
    return pl.pallas_call(
        functools.partial(_expert_kernel, nf=nf),
        grid_spec=grid_spec,
        out_shape=jax.ShapeDtypeStruct((P * SUBLANE, LANE), U32),
        compiler_params=_cparams("arbitrary", "arbitrary"),
        name="moe_experts",
    )(block_e, n_used, xs, wg, bg, wu, bu, wd, bd)


def _combine_kernel(pos_ref, ys_ref, x1_ref, gate_ref, p_ref, gp_ref, wpg_ref, wpp_ref, gf_ref, o_ref,
                    b0, b1, b2, b3, x2_scr, sem, *, tm, final):
    bufs = (b0, b1, b2, b3)

    def body(r, c):
        for k in range(TOP_K):
            pltpu.make_async_copy(_row_tile(ys_ref, pos_ref[TOP_K * r + k]), _row_tile(bufs[k], r), sem).start()
        return c

    lax.fori_loop(0, tm, body, 0)
    for k in range(TOP_K):
        pltpu.make_async_copy(ys_ref.at[pl.ds(0, tm * SUBLANE), :], bufs[k], sem).wait()

    gates = gate_ref[...]
    x2_scr[...] = x1_ref[...]
    for k in range(TOP_K):
        g = gates[:, k:k + 1]
        for col, blk in _load_packed_rows(bufs[k], tm):
            x2_scr[:, col:col + LANE] += g * blk
    x2 = x2_scr[...]
    h3 = _rms(x2, gp_ref[...]).astype(BF16)
    gt = _sigmoid(jnp.dot(h3, wpg_ref[...], preferred_element_type=F32))
    pe = jnp.dot(p_ref[...].astype(BF16), wpp_ref[...], preferred_element_type=F32)
    x3 = x2 + gt * pe
    o_ref[...] = _rms(x3, gf_ref[...]) if final else x3


def _combine(pos, ys, x1, gates, p, gp, wpg, wpp, gf, *, tok_off, n_tok, final):
    D = x1.shape[1]
    tm = 256
    off = tok_off // tm
    return pl.pallas_call(
        functools.partial(_combine_kernel, tm=tm, final=final),
        grid=(n_tok // tm,),
        in_specs=[pl.BlockSpec((TOP_K * tm,), lambda i: (off + i,), memory_space=pltpu.SMEM),
                  pl.BlockSpec(memory_space=pl.ANY),
                  pl.BlockSpec((tm, D), lambda i: (off + i, 0)),
                  pl.BlockSpec((tm, LANE), lambda i: (off + i, 0)),
                  pl.BlockSpec((tm, p.shape[1]), lambda i: (off + i, 0)),
                  _resident((1, D)), _resident(wpg.shape), _resident(wpp.shape), _resident((1, D))],
        out_specs=pl.BlockSpec((tm, D), lambda i: (i, 0)),
        out_shape=jax.ShapeDtypeStruct((n_tok, D), F32),
        scratch_shapes=[pltpu.VMEM((tm * SUBLANE, LANE), U32)] * TOP_K
                       + [pltpu.VMEM((tm, D), F32), pltpu.SemaphoreType.DMA],
        compiler_params=_cparams("arbitrary"),
        name="moe_combine_ple",
    )(pos, ys, x1, gates, p, gp, wpg, wpp, gf)


def _route(ridx, n_exp):
    flat_e = ridx.reshape(-1)
    tk = flat_e.shape[0]
    oh = (flat_e[:, None] == jnp.arange(n_exp, dtype=I32)[None, :]).astype(I32)
    csum = jnp.cumsum(oh, axis=0)
    rank = jnp.sum(oh * csum, axis=1) - 1
    counts = csum[-1]
    pcounts = (counts + MOE_BM - 1) // MOE_BM * MOE_BM
    pend = jnp.cumsum(pcounts)
    pstart = pend - pcounts
    dest = (pstart[flat_e] + rank).astype(I32)
    nblk = (tk + n_exp * (MOE_BM - 1) + MOE_BM - 1) // MOE_BM
    block_e = jnp.minimum(
        jnp.searchsorted(pend, jnp.arange(nblk, dtype=I32) * MOE_BM, side="right"), n_exp - 1).astype(I32)
    n_used = (pend[-1:] // MOE_BM).astype(I32)
    return dest, block_e, n_used, nblk * MOE_BM


def _swap_halves(w):
    half = w.shape[-1] // 2
    return jnp.concatenate([w[..., half:], w[..., :half]], axis=-1)


def _rope_tables(length):
    pos = jnp.arange(length, dtype=F32)
    inv_freq = ROPE_THETA ** (-jnp.arange(0, ROPE, 2, dtype=F32) / ROPE)
    ang = pos[:, None] * inv_freq[None, :]
    cos, sin = jnp.cos(ang), jnp.sin(ang)
    zeros = jnp.zeros((length, LANE - ROPE), F32)
    return (jnp.concatenate([cos, cos, zeros], axis=-1), jnp.concatenate([-sin, sin, zeros], axis=-1))


def kernel(x_prompt, x_sample, p_prompt, p_sample, g_mix, w_in, g_q_lat, w_q_up, g_kv_lat, w_kv_up, na_rpb, w_br_a, w_br_b, w_out, g_moe, w_router, b_router, w_gate, b_gate, w_up, b_up, w_down, b_down, g_ple, w_ple_gate, w_ple_proj, g_final):
    bp, lp, D = x_prompt.shape
    bs, ls, _ = x_sample.shape
    tp, ts = bp * lp, bs * ls
    groups = ((bp, lp), (bs, ls))
    depth = w_in.shape[0]
    na_heads = na_rpb.shape[1]
    na_w = na_heads * NA_DH
    ql = g_q_lat.shape[-1]
    kvl = g_kv_lat.shape[-1]
    mla_heads = w_q_up.shape[-1] // (NOPE + ROPE)
    n_exp = w_router.shape[-1]
    assert na_w == D and ql == kvl and lp >= ls

    x = jnp.concatenate([x_prompt.reshape(tp, D), x_sample.reshape(ts, D)], axis=0)
    cos_t, sin_t = _rope_tables(lp)
    outs = None
    for li in range(depth):
        p = jnp.concatenate([p_prompt[li].reshape(tp, -1), p_sample[li].reshape(ts, -1)], axis=0)
        wi = w_in[li]
        o = [0]
        for wdt in (na_w, na_w, na_w, ql, kvl, ROPE, D, D):
            o.append(o[-1] + wdt)
        w_qa, w_ka, w_va, w_cq, w_ckv, w_kr, w_ga, w_gb = (wi[:, o[k]:o[k + 1]] for k in range(8))
        w_main = jnp.concatenate([w_qa * (NA_DH ** -0.5), w_ka, w_va, w_ga, w_gb, w_cq, w_ckv],
                                 axis=1).astype(BF16)
        ga_col = 3 * na_w
        cq_col = 3 * na_w + 2 * D
        w_kr2 = jnp.concatenate([w_kr, _swap_halves(w_kr)], axis=1).astype(BF16)
        wq3 = w_q_up[li].reshape(ql, mla_heads, NOPE + ROPE) * ((NOPE + ROPE) ** -0.5 * LOG2E)
        wq = jnp.concatenate([wq3[..., :NOPE], wq3[..., NOPE:], _swap_halves(wq3[..., NOPE:])],
                             axis=-1).reshape(ql, mla_heads * HEAD_PAD).astype(BF16)
        wkv = w_kv_up[li].astype(BF16)
        bias = _na_bias(na_rpb[li])
        wr = jnp.pad(w_router[li], ((0, 0), (0, LANE - n_exp))).astype(BF16)
        br = jnp.pad(b_router[li].astype(F32), (0, LANE - n_exp), constant_values=NEG)[None, :]

        z, kr = _inproj(x, g_mix[li][None, :], w_main, w_kr2)
        qp, kp, vp = _mlaproj(z, kr, cos_t, sin_t, g_q_lat[li][None, :], g_kv_lat[li][None, :], wq, wkv,
                              cq_col=cq_col, groups=groups, heads=mla_heads)
        ya = _na_attn(z, bias, groups=groups, heads=na_heads, k_col=na_w, v_col=2 * na_w)
        yb = jnp.concatenate([
            _mla_attn(qp, kp, vp, batch=bp, length=lp, tok_off=0, heads=mla_heads),
            _mla_attn(qp, kp, vp, batch=bs, length=ls, tok_off=tp, heads=mla_heads)], axis=0)
        x1, h2, ridx, rgate = _merge(ya, yb, z, x, w_br_a[li].astype(BF16), w_br_b[li].astype(BF16),
                                     w_out[li].astype(BF16), g_moe[li][None, :], wr, br, ga_col=ga_col)

        dest, block_e, n_used, n_rows = _route(ridx[:, :TOP_K], n_exp)
        xs = _dispatch(dest, h2, n_rows)
        ys = _experts(block_e, n_used, xs,
                      w_gate[li].astype(BF16), b_gate[li][:, None, :], w_up[li].astype(BF16),
                      b_up[li][:, None, :], w_down[li].astype(BF16), b_down[li][:, None, :])
        final = li == depth - 1
        comb = functools.partial(_combine, dest, ys, x1, rgate, p, g_ple[li][None, :],
                                 w_ple_gate[li].astype(BF16), w_ple_proj[li].astype(BF16),
                                 g_final[None, :], final=final)
        outs = (comb(tok_off=0, n_tok=tp), comb(tok_off=tp, n_tok=ts))
        if not final:
            x = jnp.concatenate(outs, axis=0)
    return (outs[0].reshape(bp, lp, D), outs[1].reshape(bs, ls, D))
```

```python
import functools

import jax
import jax.numpy as jnp
from jax import lax
from jax.experimental import pallas as pl
from jax.experimental.pallas import tpu as pltpu

F32 = jnp.float32
BF16 = jnp.bfloat16
I32 = jnp.int32
U32 = jnp.uint32

LANE = 128
SUBLANE = 8
ROW_WORDS = SUBLANE * LANE
VMEM_LIMIT_BYTES = 56 * 1024 * 1024

RMS_EPS = 1e-6
GRID_W = 64
NA_KH = 8
NA_KW = 16
NA_DH = 128
NA_QROWS = 8
NA_QTOK = NA_QROWS * GRID_W
NA_WTOK = 2 * NA_QTOK
NA_PIECE = NA_WTOK // 4
NOPE = 128
ROPE = 64
V_DIM = 128
HEAD_PAD = 256
V_PAD = 144
ROPE_THETA = 10000.0
TOP_K = 4
SWIGLU_LIMIT = 7.0
SWIGLU_ALPHA = 1.702
MOE_BM = 512
MLA_TQ_CANDS = (2048, 1024, 512)
MLA_TK_CANDS = (512, 256)
LOG2E = 1.4426950408889634
NEG = -1e30


def _cparams(*sem):
    return pltpu.CompilerParams(dimension_semantics=sem, vmem_limit_bytes=VMEM_LIMIT_BYTES)


def _pick(n, cands):
    for c in cands:
        if n % c == 0:
            return c
    raise ValueError(f"no tile in {cands} divides {n}")


def _resident(shape):
    nd = len(shape)
    return pl.BlockSpec(shape, lambda *_: (0,) * nd, pipeline_mode=pl.Buffered(1))


def _rms(x, g):
    return x * lax.rsqrt(jnp.mean(x * x, axis=-1, keepdims=True) + RMS_EPS) * g


def _sigmoid(x):
    return 1.0 / (1.0 + jnp.exp(-x))


def _pack_pair(lo, hi):
    lo_bits = lax.bitcast_convert_type(lo.astype(BF16).astype(F32), U32) >> 16
    hi_bits = lax.bitcast_convert_type(hi.astype(BF16).astype(F32), U32)
    return lo_bits | hi_bits


def _unpack_pair(w):
    lo = lax.bitcast_convert_type(w << 16, F32)
    hi = lax.bitcast_convert_type(w & jnp.uint32(0xFFFF0000), F32)
    return lo, hi


def _store_packed_rows(ref, x):
    rows, d = x.shape
    assert d == 2 * ROW_WORDS
    for i in range(SUBLANE):
        lo = x[:, i * LANE:(i + 1) * LANE]
        hi = x[:, d // 2 + i * LANE:d // 2 + (i + 1) * LANE]
        ref[pl.ds(i, rows, stride=SUBLANE), :] = _pack_pair(lo, hi)


def _load_packed_rows(ref, rows):
    half = ROW_WORDS
    for i in range(SUBLANE):
        lo, hi = _unpack_pair(ref[pl.ds(i, rows, stride=SUBLANE), :])
        yield i * LANE, lo
        yield half + i * LANE, hi


def _group_specs(block, n_first, **kw):
    first = pl.BlockSpec(block, lambda i, *_: (jnp.minimum(i, n_first - 1), 0), **kw)
    second = pl.BlockSpec(block, lambda i, *_: (jnp.maximum(i - n_first, 0), 0), **kw)
    return [first, second]


def _inproj_kernel(xa_ref, xb_ref, g_ref, w_ref, wkr_ref, z_ref, kr_ref, h_scr, *, n_first):
    @pl.when(pl.program_id(1) == 0)
    def _():
        x = jnp.where(pl.program_id(0) < n_first, xa_ref[...], xb_ref[...])
        hb = _rms(x, g_ref[...]).astype(BF16)
        h_scr[...] = hb
        kr_ref[...] = jnp.dot(hb, wkr_ref[...], preferred_element_type=F32)

    z_ref[...] = jnp.dot(h_scr[...], w_ref[...], preferred_element_type=F32).astype(z_ref.dtype)


def _inproj(xa, xb, g, w, wkr):
    D = xa.shape[1]
    T = xa.shape[0] + xb.shape[0]
    N = w.shape[1]
    tm = _pick(xb.shape[0], (1024, 512, 256))
    assert xa.shape[0] % tm == 0
    tn = _pick(N, (1024, 512, 256, 128))
    n_first = xa.shape[0] // tm
    return pl.pallas_call(
        functools.partial(_inproj_kernel, n_first=n_first),
        grid=(T // tm, N // tn),
        in_specs=_group_specs((tm, D), n_first, pipeline_mode=pl.Buffered(1)) + [
                  pl.BlockSpec((1, D), lambda i, j: (0, 0)),
                  pl.BlockSpec((D, tn), lambda i, j: (0, j)),
                  pl.BlockSpec((D, LANE), lambda i, j: (0, 0))],
        out_specs=[pl.BlockSpec((tm, tn), lambda i, j: (i, j)),
                   pl.BlockSpec((tm, LANE), lambda i, j: (i, 0))],
        out_shape=[jax.ShapeDtypeStruct((T, N), BF16), jax.ShapeDtypeStruct((T, LANE), F32)],
        scratch_shapes=[pltpu.VMEM((tm, D), BF16)],
        compiler_params=_cparams("parallel", "arbitrary"),
        name="inproj",
    )(xa, xb, g, w, wkr)


def _mlaproj_kernel(cq_ref, ckv_ref, kr_ref, cos_ref, sin_ref, cost_ref, sint_ref, gq_ref, gkv_ref,
                    wqt_ref, wk_ref, wvt_ref, qt_out, k_out, vt_out, *, heads):
    tm = cq_ref.shape[0]
    cqn = _rms(cq_ref[...].astype(F32), gq_ref[...]).astype(BF16)
    ckvn = _rms(ckv_ref[...].astype(F32), gkv_ref[...]).astype(BF16)
    ckvn_t = ckvn.T
    qt = jnp.dot(wqt_ref[...], cqn.T, preferred_element_type=F32)
    kn = jnp.dot(ckvn, wk_ref[...], preferred_element_type=F32)
    vt = jnp.dot(wvt_ref[...], ckvn_t, preferred_element_type=F32)
    kr = kr_ref[...]
    krope = (kr * cos_ref[...] + pltpu.roll(kr, ROPE, axis=1) * sin_ref[...]).astype(BF16)
    cos_t = cost_ref[...]
    sin_t = sint_ref[...]
    tail = (lax.broadcasted_iota(I32, (V_PAD - V_DIM, tm), 0) == 0).astype(BF16)
    for h in range(heads):
        c = h * HEAD_PAD
        qt_out[c:c + NOPE, :] = qt[c:c + NOPE].astype(BF16)
        x = qt[c + NOPE:c + NOPE + ROPE]
        x_swapped = qt[c + NOPE + ROPE:c + HEAD_PAD]
        qt_out[c + NOPE:c + NOPE + ROPE, :] = (x * cos_t + x_swapped * sin_t).astype(BF16)
        qt_out[c + NOPE + ROPE:c + HEAD_PAD, :] = jnp.zeros((ROPE, tm), BF16)
        k_out[:, c:c + NOPE] = kn[:, h * NOPE:(h + 1) * NOPE].astype(BF16)
        k_out[:, c + NOPE:c + HEAD_PAD] = krope
        vt_out[h * V_PAD:h * V_PAD + V_DIM, :] = vt[h * V_DIM:(h + 1) * V_DIM].astype(BF16)
        vt_out[h * V_PAD + V_DIM:(h + 1) * V_PAD, :] = tail


def _mlaproj(z, kr, tabs, gq, gkv, wqt, wk, wvt, *, cq_col, groups, heads):
    T = z.shape[0]
    ql = gq.shape[-1]
    kvl = gkv.shape[-1]
    W = heads * HEAD_PAD
    (bp, lp), (bs, ls) = groups
    tm = _pick(ls, (256, 128))
    n_p = bp * lp // tm
    cos_n, sin_n, cos_t, sin_t = tabs

    def pos_blk(i):
        return jnp.where(i < n_p, i % (lp // tm), (i - n_p) % (ls // tm))

    return pl.pallas_call(
        functools.partial(_mlaproj_kernel, heads=heads),
        grid=(T // tm,),
        in_specs=[pl.BlockSpec((tm, ql), lambda i: (i, cq_col // ql)),
                  pl.BlockSpec((tm, kvl), lambda i: (i, (cq_col + ql) // kvl)),
                  pl.BlockSpec((tm, LANE), lambda i: (i, 0)),
                  pl.BlockSpec((tm, LANE), lambda i: (pos_blk(i), 0)),
                  pl.BlockSpec((tm, LANE), lambda i: (pos_blk(i), 0)),
                  pl.BlockSpec((ROPE, tm), lambda i: (0, pos_blk(i))),
                  pl.BlockSpec((ROPE, tm), lambda i: (0, pos_blk(i))),
                  _resident((1, ql)), _resident((1, kvl)),
                  _resident(wqt.shape), _resident(wk.shape), _resident(wvt.shape)],
        out_specs=[pl.BlockSpec((W, tm), lambda i: (0, i)),
                   pl.BlockSpec((tm, W), lambda i: (i, 0)),
                   pl.BlockSpec((heads * V_PAD, tm), lambda i: (0, i))],
        out_shape=[jax.ShapeDtypeStruct((W, T), BF16), jax.ShapeDtypeStruct((T, W), BF16),
                   jax.ShapeDtypeStruct((heads * V_PAD, T), BF16)],
        compiler_params=_cparams("parallel"),
        name="mlaproj",
    )(z, z, kr, cos_n, sin_n, cos_t, sin_t, gq, gkv, wqt, wk, wvt)


def _mla_attn_kernel(q_ref, k_ref, v_ref, o_ref, s0, s1, p0, p1, a0, a1, m_scr, acc_scr, *, tk, nk):
    qt = q_ref[...]

    def qk(c, s_ref):
        off = pl.multiple_of(c * tk, tk)
        s_ref[...] = jnp.dot(k_ref[pl.ds(off, tk), :], qt, preferred_element_type=F32)

    def sm(s_ref, p_ref, a_ref):
        s = s_ref[...]
        m_prev = m_scr[...]
        m_new = jnp.maximum(m_prev, jnp.max(s, axis=0, keepdims=True))
        a_ref[...] = jnp.exp2(m_prev - m_new)
        p_ref[...] = jnp.exp2(s - m_new).astype(BF16)
        m_scr[...] = m_new

    def pv(c, p_ref, a_ref):
        off = pl.multiple_of(c * tk, tk)
        acc_scr[...] = acc_scr[...] * a_ref[...] + jnp.dot(v_ref[:, pl.ds(off, tk)], p_ref[...],
                                                           preferred_element_type=F32)

    m_scr[...] = jnp.full(m_scr.shape, NEG, F32)
    acc_scr[...] = jnp.zeros(acc_scr.shape, F32)
    qk(0, s0)
    qk(1, s1)
    sm(s0, p0, a0)

    def body(j, carry):
        qk(2 * j, s0)
        sm(s1, p1, a1)
        pv(2 * j - 2, p0, a0)
        qk(2 * j + 1, s1)
        sm(s0, p0, a0)
        pv(2 * j - 1, p1, a1)
        return carry

    lax.fori_loop(1, nk // 2, body, 0)
    sm(s1, p1, a1)
    pv(nk - 2, p0, a0)
    pv(nk - 1, p1, a1)
    acc = acc_scr[...]
    o_ref[...] = (acc[:V_DIM] / acc[V_DIM:V_DIM + 1]).T.astype(o_ref.dtype)


def _mla_attn(qt, kp, vt, *, batch, length, tok_off, heads):
    tq = _pick(length, MLA_TQ_CANDS)
    tk = _pick(length // 2, MLA_TK_CANDS)
    nk = length // tk
    assert nk % 2 == 0 and tok_off % length == 0
    nq = length // tq
    qoff = tok_off // tq
    boff = tok_off // length
    return pl.pallas_call(
        functools.partial(_mla_attn_kernel, tk=tk, nk=nk),
        grid=(batch, heads, nq),
        in_specs=[pl.BlockSpec((HEAD_PAD, tq), lambda b, h, i: (h, qoff + b * nq + i)),
                  pl.BlockSpec((length, HEAD_PAD), lambda b, h, i: (boff + b, h)),
                  pl.BlockSpec((V_PAD, length), lambda b, h, i: (h, boff + b))],
        out_specs=pl.BlockSpec((tq, V_DIM), lambda b, h, i: (b * nq + i, h)),
        out_shape=jax.ShapeDtypeStruct((batch * length, heads * V_DIM), BF16),
        scratch_shapes=[pltpu.VMEM((tk, tq), F32), pltpu.VMEM((tk, tq), F32),
                        pltpu.VMEM((tk, tq), BF16), pltpu.VMEM((tk, tq), BF16),
                        pltpu.VMEM((1, tq), F32), pltpu.VMEM((1, tq), F32),
                        pltpu.VMEM((1, tq), F32), pltpu.VMEM((V_PAD, tq), F32)],
        compiler_params=_cparams("parallel", "parallel", "arbitrary"),
        name="mla_attn",
    )(qt, kp, vt)


def _na_kernel(q_ref, k0, k1, k2, k3, v0, v1, v2, v3, b_ref, o_ref):
    q = q_ref[...]
    pw = NA_PIECE
    ss = []
    for j, kr in enumerate((k0, k1, k2, k3)):
        s = lax.dot_general(q, kr[...], (((1,), (1,)), ((), ())), preferred_element_type=F32)
        ss.append(s + b_ref[:, j * pw:(j + 1) * pw])
    m = functools.reduce(jnp.maximum, [jnp.max(s, axis=-1, keepdims=True) for s in ss])
    acc = None
    den = None
    for s, vr in zip(ss, (v0, v1, v2, v3)):
        p = jnp.exp2(s - m)
        d = jnp.sum(p, axis=-1, keepdims=True)
        a = jnp.dot(p.astype(BF16), vr[...], preferred_element_type=F32)
        acc = a if acc is None else acc + a
        den = d if den is None else den + d
    o_ref[...] = (acc / den).astype(o_ref.dtype)


def _na_bias(rpb):
    H = rpb.shape[0]
    a = jnp.arange(NA_QROWS)
    j = jnp.arange(2 * NA_QROWS)
    c = jnp.arange(GRID_W)
    q_rel = jnp.stack([a, a + NA_KH // 2, a + NA_QROWS])
    rs_rel = jnp.stack([jnp.maximum(a - NA_KH // 2, 0), a, jnp.minimum(a + NA_KH // 2, NA_QROWS)])
    vrow = (j[None, None, :] >= rs_rel[:, :, None]) & (j[None, None, :] < rs_rel[:, :, None] + NA_KH)
    dy = jnp.clip(j[None, None, :] - q_rel[:, :, None] + NA_KH - 1, 0, 2 * NA_KH - 2)
    cs = jnp.clip(c - NA_KW // 2, 0, GRID_W - NA_KW)
    vcol = (c[None, :] >= cs[:, None]) & (c[None, :] < cs[:, None] + NA_KW)
    dx = jnp.clip(c[None, :] - c[:, None] + NA_KW - 1, 0, 2 * NA_KW - 2)
    oh_x = (dx[:, :, None] == jnp.arange(2 * NA_KW - 1)[None, None, :]).astype(F32)
    oh_y = (dy[..., None] == jnp.arange(2 * NA_KH - 1)[None, None, None, :]).astype(F32)
    b2 = jnp.einsum("hyx,cdx->hycd", rpb.astype(F32), oh_x, precision=lax.Precision.HIGHEST)
    b = jnp.einsum("kajy,hycd->hkacjd", oh_y, b2, precision=lax.Precision.HIGHEST)
    valid = vrow[:, :, None, :, None] & vcol[None, None, :, None, :]
    b = jnp.where(valid[None], b * LOG2E, NEG)
    return b.transpose(1, 0, 2, 3, 4, 5).reshape(3, H, NA_QTOK, NA_WTOK)


def _na_attn(z, bias, *, groups, heads, k_col, v_col):
    T = z.shape[0]
    (bp, lp), (bs, ls) = groups
    n_p = bp * lp // NA_QTOK
    nbp = lp // NA_QTOK
    nbs = ls // NA_QTOK
    assert nbp >= 2 and nbs >= 2, "needs at least 16 grid rows per sequence"
    piece_per_blk = NA_QTOK // NA_PIECE

    def meta(g):
        is_p = g < n_p
        gl = jnp.where(is_p, g, g - n_p)
        nb = jnp.where(is_p, nbp, nbs)
        seq = gl // nb
        i = gl % nb
        base = jnp.where(is_p, 0, bp * lp // NA_PIECE) + seq * (piece_per_blk * nb)
        w0 = jnp.clip(piece_per_blk * i - 1, 0, piece_per_blk * nb - 4)
        kind = jnp.where(i == 0, 0, jnp.where(i == nb - 1, 2, 1))
        return base + w0, kind

    def piece_spec(col, jj):
        return pl.BlockSpec((NA_PIECE, NA_DH), lambda h, g: (meta(g)[0] + jj, col + h))

    return pl.pallas_call(
        _na_kernel,
        grid=(heads, T // NA_QTOK),
        in_specs=([pl.BlockSpec((NA_QTOK, NA_DH), lambda h, g: (g, h))]
                  + [piece_spec(k_col // NA_DH, jj) for jj in range(4)]
                  + [piece_spec(v_col // NA_DH, jj) for jj in range(4)]
                  + [pl.BlockSpec((None, None, NA_QTOK, NA_WTOK), lambda h, g: (meta(g)[1], h, 0, 0))]),
        out_specs=pl.BlockSpec((NA_QTOK, NA_DH), lambda h, g: (g, h)),
        out_shape=jax.ShapeDtypeStruct((T, heads * NA_DH), BF16),
        compiler_params=_cparams("parallel", "parallel"),
        name="na_attn",
    )(z, *([z] * 8), bias)


def _merge_kernel(ya_ref, yb0_ref, yb1_ref, ga_ref, gb_ref, x0_ref, x1in_ref, wa_ref, wb_ref, wo_ref, gm_ref,
                  wr_ref, br_ref, x1_ref, h2_ref, ridx_ref, rgate_ref, *, n_first):
    in_first = pl.program_id(0) < n_first
    ya = jnp.dot(ya_ref[...], wa_ref[...], preferred_element_type=F32)
    yb = jnp.dot(jnp.where(in_first, yb0_ref[...], yb1_ref[...]), wb_ref[...], preferred_element_type=F32)
    u = _sigmoid(ga_ref[...].astype(F32)) * ya + _sigmoid(gb_ref[...].astype(F32)) * yb
    x = jnp.where(in_first, x0_ref[...], x1in_ref[...])
    x1 = x + jnp.dot(u.astype(BF16), wo_ref[...], preferred_element_type=F32)
    x1_ref[...] = x1
    h2 = _rms(x1, gm_ref[...])
    _store_packed_rows(h2_ref, h2)
    logits = jnp.dot(h2.astype(BF16), wr_ref[...], preferred_element_type=F32) + br_ref[...]
    lane = lax.broadcasted_iota(I32, logits.shape, 1)
    vals, idxs = [], []
    for _ in range(TOP_K):
        mx = jnp.max(logits, axis=-1, keepdims=True)
        ix = jnp.min(jnp.where(logits == mx, lane, LANE), axis=-1, keepdims=True)
        vals.append(mx)
        idxs.append(ix)
        logits = jnp.where(lane == ix, 2 * NEG, logits)
    es = [jnp.exp(v - vals[0]) for v in vals]
    den = functools.reduce(lambda a, b: a + b, es)
    ridx = jnp.zeros(lane.shape, I32)
    rgate = jnp.zeros(lane.shape, F32)
    for k in range(TOP_K):
        ridx = jnp.where(lane == k, idxs[k], ridx)
        rgate = jnp.where(lane == k, es[k] / den, rgate)
    ridx_ref[...] = ridx
    rgate_ref[...] = rgate


def _merge(ya, yb0, yb1, z, x0, x1, wa, wb, wo, gm, wr, br, *, ga_col):
    D = x0.shape[1]
    T = x0.shape[0] + x1.shape[0]
    tm = 256
    n_first = x0.shape[0] // tm
    return pl.pallas_call(
        functools.partial(_merge_kernel, n_first=n_first),
        grid=(T // tm,),
        in_specs=[pl.BlockSpec((tm, D), lambda i: (i, 0))]
                 + _group_specs((tm, D), n_first)
                 + [pl.BlockSpec((tm, D), lambda i: (i, ga_col // D)),
                    pl.BlockSpec((tm, D), lambda i: (i, ga_col // D + 1))]
                 + _group_specs((tm, D), n_first)
                 + [_resident(wa.shape), _resident(wb.shape), _resident(wo.shape),
                    _resident((1, D)), _resident((D, LANE)), _resident((1, LANE))],
        out_specs=[pl.BlockSpec((tm, D), lambda i: (i, 0)),
                   pl.BlockSpec((tm * SUBLANE, LANE), lambda i: (i, 0)),
                   pl.BlockSpec((tm, LANE), lambda i: (i, 0)),
                   pl.BlockSpec((tm, LANE), lambda i: (i, 0))],
        out_shape=[jax.ShapeDtypeStruct((T, D), F32), jax.ShapeDtypeStruct((T * SUBLANE, LANE), U32),
                   jax.ShapeDtypeStruct((T, LANE), I32), jax.ShapeDtypeStruct((T, LANE), F32)],
        compiler_params=_cparams("parallel"),
        name="merge_router",
    )(ya, yb0, yb1, z, z, x0, x1, wa, wb, wo, gm, wr, br)


def _row_tile(ref, r):
    return ref.at[pl.ds(pl.multiple_of(r * SUBLANE, SUBLANE), SUBLANE), :]


def _dispatch_kernel(dest_ref, h_ref, xs_in_ref, xs_ref, sem, *, tm):
    del xs_in_ref

    def body(r, c):
        for k in range(TOP_K):
            pltpu.make_async_copy(_row_tile(h_ref, r), _row_tile(xs_ref, dest_ref[TOP_K * r + k]), sem).start()
        return c

    lax.fori_loop(0, tm, body, 0)
    for _ in range(TOP_K):
        pltpu.make_async_copy(h_ref, xs_ref.at[pl.ds(0, tm * SUBLANE), :], sem).wait()


def _dispatch(dest, h2p, n_rows):
    T = h2p.shape[0] // SUBLANE
    tm = 256
    xs0 = jnp.zeros((n_rows * SUBLANE, LANE), U32)
    return pl.pallas_call(
        functools.partial(_dispatch_kernel, tm=tm),
        grid=(T // tm,),
        in_specs=[pl.BlockSpec((TOP_K * tm,), lambda i: (i,), memory_space=pltpu.SMEM),
                  pl.BlockSpec((tm * SUBLANE, LANE), lambda i: (i, 0)),
                  pl.BlockSpec(memory_space=pl.ANY)],
        out_specs=pl.BlockSpec(memory_space=pl.ANY),
        out_shape=jax.ShapeDtypeStruct((n_rows * SUBLANE, LANE), U32),
        scratch_shapes=[pltpu.SemaphoreType.DMA],
        input_output_aliases={2: 0},
        compiler_params=_cparams("arbitrary"),
        name="moe_dispatch",
    )(dest, h2p, xs0)


def _expert_kernel(be_ref, nu_ref, xs_ref, wg_ref, bg_ref, wu_ref, bu_ref, wd_ref, bd_ref, ys_ref,
                   xb_scr, acc_scr, *, nf):
    i = pl.program_id(0)
    f = pl.program_id(1)
    bm = xb_scr.shape[0]
    used = i < nu_ref[0]

    @pl.when(used)
    def _():
        @pl.when(f == 0)
        def _():
            for col, blk in _load_packed_rows(xs_ref, bm):
                xb_scr[:, col:col + LANE] = blk.astype(BF16)

        xb = xb_scr[...]
        g = jnp.dot(xb, wg_ref[0], preferred_element_type=F32) + bg_ref[0]
        u = jnp.dot(xb, wu_ref[0], preferred_element_type=F32) + bu_ref[0]
        g = jnp.minimum(g, SWIGLU_LIMIT)
        u = jnp.clip(u, -SWIGLU_LIMIT, SWIGLU_LIMIT)
        a = g * _sigmoid(SWIGLU_ALPHA * g) * (u + 1.0)
        part = jnp.dot(a.astype(BF16), wd_ref[0], preferred_element_type=F32)

        @pl.when(f == 0)
        def _():
            acc_scr[...] = part + bd_ref[0]

        @pl.when(f > 0)
        def _():
            acc_scr[...] += part

    @pl.when(f == nf - 1)
    def _():
        @pl.when(used)
        def _():
            _store_packed_rows(ys_ref, acc_scr[...])

        @pl.when(jnp.logical_not(used))
        def _():
            ys_ref[...] = jnp.zeros(ys_ref.shape, U32)


def _experts(block_e, n_used, xs, wg, bg, wu, bu, wd, bd):
    P = xs.shape[0] // SUBLANE
    E, D, F = wg.shape
    tf = _pick(F, (1024, 512, 256, 128))
    nf = F // tf
    nblk = P // MOE_BM

    def blk(i, nu):
        return jnp.minimum(i, nu[0] - 1)

    def fidx(i, f, nu):
        return jnp.where(i < nu[0], f, nf - 1)

    grid_spec = pltpu.PrefetchScalarGridSpec(
        num_scalar_prefetch=2,
        grid=(nblk, nf),
        in_specs=[pl.BlockSpec((MOE_BM * SUBLANE, LANE), lambda i, f, be, nu: (blk(i, nu), 0)),
                  pl.BlockSpec((1, D, tf), lambda i, f, be, nu: (be[blk(i, nu)], 0, fidx(i, f, nu))),
                  pl.BlockSpec((1, 1, tf), lambda i, f, be, nu: (be[blk(i, nu)], 0, fidx(i, f, nu))),
                  pl.BlockSpec((1, D, tf), lambda i, f, be, nu: (be[blk(i, nu)], 0, fidx(i, f, nu))),
                  pl.BlockSpec((1, 1, tf), lambda i, f, be, nu: (be[blk(i, nu)], 0, fidx(i, f, nu))),
                  pl.BlockSpec((1, tf, D), lambda i, f, be, nu: (be[blk(i, nu)], fidx(i, f, nu), 0)),
                  pl.BlockSpec((1, 1, D), lambda i, f, be, nu: (be[blk(i, nu)], 0, 0))],
        out_specs=pl.BlockSpec((MOE_BM * SUBLANE, LANE), lambda i, f, be, nu: (i, 0)),
        scratch_shapes=[pltpu.VMEM((MOE_BM, D), BF16), pltpu.VMEM((MOE_BM, D), F32)],
    )
    return pl.pallas_call(
        functools.partial(_expert_kernel, nf=nf),
        grid_spec=grid_spec,
        out_shape=jax.ShapeDtypeStruct((P * SUBLANE, LANE), U32),
        compiler_params=_cparams("arbitrary", "arbitrary"),
        name="moe_experts",
    )(block_e, n_used, xs, wg, bg, wu, bu, wd, bd)


def _combine_kernel(pos_ref, ys_ref, x1_ref, gate_ref, p_ref, gp_ref, wpg_ref, wpp_ref, gf_ref, o_ref,
                    b0, b1, b2, b3, x2_scr, sem, *, tm, final):
    bufs = (b0, b1, b2, b3)

    def body(r, c):
        for k in range(TOP_K):
            pltpu.make_async_copy(_row_tile(ys_ref, pos_ref[TOP_K * r + k]), _row_tile(bufs[k], r), sem).start()
        return c

    lax.fori_loop(0, tm, body, 0)
    for k in range(TOP_K):
        pltpu.make_async_copy(ys_ref.at[pl.ds(0, tm * SUBLANE), :], bufs[k], sem).wait()

    gates = gate_ref[...]
    x2_scr[...] = x1_ref[...]
    for k in range(TOP_K):
        g = gates[:, k:k + 1]
        for col, blk in _load_packed_rows(bufs[k], tm):
            x2_scr[:, col:col + LANE] += g * blk
    x2 = x2_scr[...]
    h3 = _rms(x2, gp_ref[...]).astype(BF16)
    gt = _sigmoid(jnp.dot(h3, wpg_ref[...], preferred_element_type=F32))
    pe = jnp.dot(p_ref[...].astype(BF16), wpp_ref[...], preferred_element_type=F32)
    x3 = x2 + gt * pe
    o_ref[...] = _rms(x3, gf_ref[...]) if final else x3


def _combine(pos, ys, x1, gates, p, gp, wpg, wpp, gf, *, tok_off, n_tok, final):
    D = x1.shape[1]
    tm = 256
    off = tok_off // tm
    return pl.pallas_call(
        functools.partial(_combine_kernel, tm=tm, final=final),
        grid=(n_tok // tm,),
        in_specs=[pl.BlockSpec((TOP_K * tm,), lambda i: (off + i,), memory_space=pltpu.SMEM),
                  pl.BlockSpec(memory_space=pl.ANY),
                  pl.BlockSpec((tm, D), lambda i: (off + i, 0)),
                  pl.BlockSpec((tm, LANE), lambda i: (off + i, 0)),
                  pl.BlockSpec((tm, p.shape[1]), lambda i: (off + i, 0)),
                  _resident((1, D)), _resident(wpg.shape), _resident(wpp.shape), _resident((1, D))],
        out_specs=pl.BlockSpec((tm, D), lambda i: (i, 0)),
        out_shape=jax.ShapeDtypeStruct((n_tok, D), F32),
        scratch_shapes=[pltpu.VMEM((tm * SUBLANE, LANE), U32)] * TOP_K
                       + [pltpu.VMEM((tm, D), F32), pltpu.SemaphoreType.DMA],
        compiler_params=_cparams("arbitrary"),
        name="moe_combine_ple",
    )(pos, ys, x1, gates, p, gp, wpg, wpp, gf)


def _route(ridx, n_exp):
    flat_e = ridx.reshape(-1)
    tk = flat_e.shape[0]
    oh = (flat_e[:, None] == jnp.arange(n_exp, dtype=I32)[None, :]).astype(I32)
    csum = jnp.cumsum(oh, axis=0)
    rank = jnp.sum(oh * csum, axis=1) - 1
    counts = csum[-1]
    pcounts = (counts + MOE_BM - 1) // MOE_BM * MOE_BM
    pend = jnp.cumsum(pcounts)
    pstart = pend - pcounts
    dest = (pstart[flat_e] + rank).astype(I32)
    nblk = (tk + n_exp * (MOE_BM - 1) + MOE_BM - 1) // MOE_BM
    blk_start = jnp.arange(nblk, dtype=I32) * MOE_BM
    block_e = jnp.minimum(jnp.sum((blk_start[:, None] >= pend[None, :]).astype(I32), axis=1), n_exp - 1)
    n_used = (pend[-1:] // MOE_BM).astype(I32)
    return dest, block_e, n_used, nblk * MOE_BM


def _swap_halves(w):
    half = w.shape[-1] // 2
    return jnp.concatenate([w[..., half:], w[..., :half]], axis=-1)


def _rope_tables(length):
    pos = jnp.arange(length, dtype=F32)
    inv_freq = ROPE_THETA ** (-jnp.arange(0, ROPE, 2, dtype=F32) / ROPE)
    ang = pos[:, None] * inv_freq[None, :]
    cos, sin = jnp.cos(ang), jnp.sin(ang)
    cos2 = jnp.concatenate([cos, cos], axis=-1)
    sin2 = jnp.concatenate([-sin, sin], axis=-1)
    zeros = jnp.zeros((length, LANE - ROPE), F32)
    return (jnp.concatenate([cos2, zeros], axis=-1), jnp.concatenate([sin2, zeros], axis=-1),
            cos2.T, sin2.T)


def kernel(x_prompt, x_sample, p_prompt, p_sample, g_mix, w_in, g_q_lat, w_q_up, g_kv_lat, w_kv_up, na_rpb, w_br_a, w_br_b, w_out, g_moe, w_router, b_router, w_gate, b_gate, w_up, b_up, w_down, b_down, g_ple, w_ple_gate, w_ple_proj, g_final):
    bp, lp, D = x_prompt.shape
    bs, ls, _ = x_sample.shape
    tp, ts = bp * lp, bs * ls
    groups = ((bp, lp), (bs, ls))
    depth = w_in.shape[0]
    na_heads = na_rpb.shape[1]
    na_w = na_heads * NA_DH
    ql = g_q_lat.shape[-1]
    kvl = g_kv_lat.shape[-1]
    mla_heads = w_q_up.shape[-1] // (NOPE + ROPE)
    n_exp = w_router.shape[-1]
    assert na_w == D and ql == kvl and lp >= ls

    xs_in = (x_prompt.reshape(tp, D), x_sample.reshape(ts, D))
    rope_tabs = _rope_tables(lp)
    outs = None
    for li in range(depth):
        p = jnp.concatenate([p_prompt[li].reshape(tp, -1), p_sample[li].reshape(ts, -1)], axis=0)
        wi = w_in[li]
        o = [0]
        for wdt in (na_w, na_w, na_w, ql, kvl, ROPE, D, D):
            o.append(o[-1] + wdt)
        w_qa, w_ka, w_va, w_cq, w_ckv, w_kr, w_ga, w_gb = (wi[:, o[k]:o[k + 1]] for k in range(8))
        w_main = jnp.concatenate([w_qa * (NA_DH ** -0.5 * LOG2E), w_ka, w_va, w_ga, w_gb, w_cq, w_ckv],
                                 axis=1).astype(BF16)
        ga_col = 3 * na_w
        cq_col = 3 * na_w + 2 * D
        w_kr2 = jnp.concatenate([w_kr, _swap_halves(w_kr)], axis=1).astype(BF16)
        wq3 = w_q_up[li].reshape(ql, mla_heads, NOPE + ROPE) * ((NOPE + ROPE) ** -0.5 * LOG2E)
        wqt = jnp.concatenate([wq3[..., :NOPE], wq3[..., NOPE:], _swap_halves(wq3[..., NOPE:])],
                              axis=-1).reshape(ql, mla_heads * HEAD_PAD).T.astype(BF16)
        wkv3 = w_kv_up[li].reshape(kvl, mla_heads, NOPE + V_DIM)
        wk = wkv3[..., :NOPE].reshape(kvl, mla_heads * NOPE).astype(BF16)
        wvt = wkv3[..., NOPE:].reshape(kvl, mla_heads * V_DIM).T.astype(BF16)
        bias = _na_bias(na_rpb[li])
        wr = jnp.pad(w_router[li], ((0, 0), (0, LANE - n_exp))).astype(BF16)
        br = jnp.pad(b_router[li].astype(F32), (0, LANE - n_exp), constant_values=NEG)[None, :]

        z, kr = _inproj(xs_in[0], xs_in[1], g_mix[li][None, :], w_main, w_kr2)
        qt, kp, vt = _mlaproj(z, kr, rope_tabs, g_q_lat[li][None, :], g_kv_lat[li][None, :], wqt, wk, wvt,
                              cq_col=cq_col, groups=groups, heads=mla_heads)
        ya = _na_attn(z, bias, groups=groups, heads=na_heads, k_col=na_w, v_col=2 * na_w)
        yb0 = _mla_attn(qt, kp, vt, batch=bp, length=lp, tok_off=0, heads=mla_heads)
        yb1 = _mla_attn(qt, kp, vt, batch=bs, length=ls, tok_off=tp, heads=mla_heads)
        x1, h2, ridx, rgate = _merge(ya, yb0, yb1, z, xs_in[0], xs_in[1], w_br_a[li].astype(BF16),
                                     w_br_b[li].astype(BF16), w_out[li].astype(BF16), g_moe[li][None, :],
                                     wr, br, ga_col=ga_col)

        dest, block_e, n_used, n_rows = _route(ridx[:, :TOP_K], n_exp)
        xs = _dispatch(dest, h2, n_rows)
        ys = _experts(block_e, n_used, xs,
                      w_gate[li].astype(BF16), b_gate[li][:, None, :], w_up[li].astype(BF16),
                      b_up[li][:, None, :], w_down[li].astype(BF16), b_down[li][:, None, :])
        final = li == depth - 1
        comb = functools.partial(_combine, dest, ys, x1, rgate, p, g_ple[li][None, :],
                                 w_ple_gate[li].astype(BF16), w_ple_proj[li].astype(BF16),
                                 g_final[None, :], final=final)
        outs = (comb(tok_off=0, n_tok=tp), comb(tok_off=tp, n_tok=ts))
        xs_in = outs
    return (outs[0].reshape(bp, lp, D), outs[1].reshape(bs, ls, D))
```

```python
import functools

import jax
import jax.numpy as jnp
from jax import lax
from jax.experimental import pallas as pl
from jax.experimental.pallas import tpu as pltpu

F32 = jnp.float32
BF16 = jnp.bfloat16
I32 = jnp.int32
U32 = jnp.uint32

LANE = 128
SUBLANE = 8
ROW_WORDS = SUBLANE * LANE
VMEM_LIMIT_BYTES = 56 * 1024 * 1024

RMS_EPS = 1e-6
GRID_W = 64
NA_KH = 8
NA_KW = 16
NA_DH = 128
NA_QROWS = 8
NA_QTOK = NA_QROWS * GRID_W
NA_WTOK = 2 * NA_QTOK
NA_PIECE = NA_WTOK // 4
NA_SM_STRIP = 128
NOPE = 128
ROPE = 64
V_DIM = 128
HEAD_PAD = 256
V_PAD = 144
ROPE_THETA = 10000.0
TOP_K = 4
SWIGLU_LIMIT = 7.0
SWIGLU_ALPHA = 1.702
MOE_BM = 512
MLA_TQ_CANDS = (2048, 1024, 512)
MLA_TK_CANDS = (512, 256)
MLA_SM_STRIP = 256
LOG2E = 1.4426950408889634
NEG = -1e30


def _cparams(*sem):
    return pltpu.CompilerParams(dimension_semantics=sem, vmem_limit_bytes=VMEM_LIMIT_BYTES)


def _pick(n, cands):
    for c in cands:
        if n % c == 0:
            return c
    raise ValueError(f"no tile in {cands} divides {n}")


def _resident(shape):
    nd = len(shape)
    return pl.BlockSpec(shape, lambda *_: (0,) * nd, pipeline_mode=pl.Buffered(1))


def _rms(x, g):
    return x * lax.rsqrt(jnp.mean(x * x, axis=-1, keepdims=True) + RMS_EPS) * g


def _sigmoid(x):
    return 1.0 / (1.0 + jnp.exp(-x))


def _pack_pair(lo, hi):
    lo_bits = lax.bitcast_convert_type(lo.astype(BF16).astype(F32), U32) >> 16
    hi_bits = lax.bitcast_convert_type(hi.astype(BF16).astype(F32), U32)
    return lo_bits | hi_bits


def _unpack_pair(w):
    lo = lax.bitcast_convert_type(w << 16, F32)
    hi = lax.bitcast_convert_type(w & jnp.uint32(0xFFFF0000), F32)
    return lo, hi


def _store_packed_rows(ref, x):
    rows, d = x.shape
    assert d == 2 * ROW_WORDS
    for i in range(SUBLANE):
        lo = x[:, i * LANE:(i + 1) * LANE]
        hi = x[:, d // 2 + i * LANE:d // 2 + (i + 1) * LANE]
        ref[pl.ds(i, rows, stride=SUBLANE), :] = _pack_pair(lo, hi)


def _load_packed_rows(ref, rows):
    half = ROW_WORDS
    for i in range(SUBLANE):
        lo, hi = _unpack_pair(ref[pl.ds(i, rows, stride=SUBLANE), :])
        yield i * LANE, lo
        yield half + i * LANE, hi


def _group_specs(block, n_first, **kw):
    first = pl.BlockSpec(block, lambda i, *_: (jnp.minimum(i, n_first - 1), 0), **kw)
    second = pl.BlockSpec(block, lambda i, *_: (jnp.maximum(i - n_first, 0), 0), **kw)
    return [first, second]


def _inproj_kernel(xa_ref, xb_ref, g_ref, w_ref, wkr_ref, z_ref, kr_ref, h_scr, *, n_first):
    @pl.when(pl.program_id(1) == 0)
    def _():
        x = jnp.where(pl.program_id(0) < n_first, xa_ref[...], xb_ref[...])
        hb = _rms(x, g_ref[...]).astype(BF16)
        h_scr[...] = hb
        kr_ref[...] = jnp.dot(hb, wkr_ref[...], preferred_element_type=F32)

    z_ref[...] = jnp.dot(h_scr[...], w_ref[...], preferred_element_type=F32).astype(z_ref.dtype)


def _inproj(xa, xb, g, w, wkr):
    D = xa.shape[1]
    T = xa.shape[0] + xb.shape[0]
    N = w.shape[1]
    tm = _pick(xb.shape[0], (1024, 512, 256))
    assert xa.shape[0] % tm == 0
    tn = _pick(N, (1024, 512, 256, 128))
    n_first = xa.shape[0] // tm
    return pl.pallas_call(
        functools.partial(_inproj_kernel, n_first=n_first),
        grid=(T // tm, N // tn),
        in_specs=_group_specs((tm, D), n_first, pipeline_mode=pl.Buffered(1)) + [
                  pl.BlockSpec((1, D), lambda i, j: (0, 0)),
                  pl.BlockSpec((D, tn), lambda i, j: (0, j)),
                  pl.BlockSpec((D, LANE), lambda i, j: (0, 0))],
        out_specs=[pl.BlockSpec((tm, tn), lambda i, j: (i, j)),
                   pl.BlockSpec((tm, LANE), lambda i, j: (i, 0))],
        out_shape=[jax.ShapeDtypeStruct((T, N), BF16), jax.ShapeDtypeStruct((T, LANE), F32)],
        scratch_shapes=[pltpu.VMEM((tm, D), BF16)],
        compiler_params=_cparams("parallel", "arbitrary"),
        name="inproj",
    )(xa, xb, g, w, wkr)


def _mlaproj_kernel(cq_ref, ckv_ref, kr_ref, cos_ref, sin_ref, cost_ref, sint_ref, gq_ref, gkv_ref,
                    wqt_ref, wk_ref, wvt_ref, qt_out, k_out, vt_out, *, heads):
    tm = cq_ref.shape[0]
    cqn = _rms(cq_ref[...].astype(F32), gq_ref[...]).astype(BF16)
    ckvn = _rms(ckv_ref[...].astype(F32), gkv_ref[...]).astype(BF16)
    ckvn_t = ckvn.T
    qt = jnp.dot(wqt_ref[...], cqn.T, preferred_element_type=F32)
    kn = jnp.dot(ckvn, wk_ref[...], preferred_element_type=F32)
    vt = jnp.dot(wvt_ref[...], ckvn_t, preferred_element_type=F32)
    kr = kr_ref[...]
    krope = (kr * cos_ref[...] + pltpu.roll(kr, ROPE, axis=1) * sin_ref[...]).astype(BF16)
    cos_t = cost_ref[...]
    sin_t = sint_ref[...]
    tail = (lax.broadcasted_iota(I32, (V_PAD - V_DIM, tm), 0) == 0).astype(BF16)
    for h in range(heads):
        c = h * HEAD_PAD
        qt_out[c:c + NOPE, :] = qt[c:c + NOPE].astype(BF16)
        x = qt[c + NOPE:c + NOPE + ROPE]
        x_swapped = qt[c + NOPE + ROPE:c + HEAD_PAD]
        qt_out[c + NOPE:c + NOPE + ROPE, :] = (x * cos_t + x_swapped * sin_t).astype(BF16)
        qt_out[c + NOPE + ROPE:c + HEAD_PAD, :] = jnp.zeros((ROPE, tm), BF16)
        k_out[:, c:c + NOPE] = kn[:, h * NOPE:(h + 1) * NOPE].astype(BF16)
        k_out[:, c + NOPE:c + HEAD_PAD] = krope
        vt_out[h * V_PAD:h * V_PAD + V_DIM, :] = vt[h * V_DIM:(h + 1) * V_DIM].astype(BF16)
        vt_out[h * V_PAD + V_DIM:(h + 1) * V_PAD, :] = tail


def _mlaproj(z, kr, tabs, gq, gkv, wqt, wk, wvt, *, cq_col, groups, heads):
    T = z.shape[0]
    ql = gq.shape[-1]
    kvl = gkv.shape[-1]
    W = heads * HEAD_PAD
    (bp, lp), (bs, ls) = groups
    tm = _pick(ls, (256, 128))
    n_p = bp * lp // tm
    cos_n, sin_n, cos_t, sin_t = tabs

    def pos_blk(i):
        return jnp.where(i < n_p, i % (lp // tm), (i - n_p) % (ls // tm))

    return pl.pallas_call(
        functools.partial(_mlaproj_kernel, heads=heads),
        grid=(T // tm,),
        in_specs=[pl.BlockSpec((tm, ql), lambda i: (i, cq_col // ql)),
                  pl.BlockSpec((tm, kvl), lambda i: (i, (cq_col + ql) // kvl)),
                  pl.BlockSpec((tm, LANE), lambda i: (i, 0)),
                  pl.BlockSpec((tm, LANE), lambda i: (pos_blk(i), 0)),
                  pl.BlockSpec((tm, LANE), lambda i: (pos_blk(i), 0)),
                  pl.BlockSpec((ROPE, tm), lambda i: (0, pos_blk(i))),
                  pl.BlockSpec((ROPE, tm), lambda i: (0, pos_blk(i))),
                  _resident((1, ql)), _resident((1, kvl)),
                  _resident(wqt.shape), _resident(wk.shape), _resident(wvt.shape)],
        out_specs=[pl.BlockSpec((W, tm), lambda i: (0, i)),
                   pl.BlockSpec((tm, W), lambda i: (i, 0)),
                   pl.BlockSpec((heads * V_PAD, tm), lambda i: (0, i))],
        out_shape=[jax.ShapeDtypeStruct((W, T), BF16), jax.ShapeDtypeStruct((T, W), BF16),
                   jax.ShapeDtypeStruct((heads * V_PAD, T), BF16)],
        compiler_params=_cparams("parallel"),
        name="mlaproj",
    )(z, z, kr, cos_n, sin_n, cos_t, sin_t, gq, gkv, wqt, wk, wvt)


def _mla_attn_kernel(q_ref, k_ref, v_ref, o_ref, s0, s1, p0, p1, a0, a1, m_scr, acc_scr, *, tk, nk):
    qt = q_ref[...]

    def qk(c, s_ref):
        off = pl.multiple_of(c * tk, tk)
        s_ref[...] = jnp.dot(k_ref[pl.ds(off, tk), :], qt, preferred_element_type=F32)

    def sm(s_ref, p_ref, a_ref):
        for j in range(s_ref.shape[1] // MLA_SM_STRIP):
            sl = slice(j * MLA_SM_STRIP, (j + 1) * MLA_SM_STRIP)
            m_prev = m_scr[:, sl]
            m_new = jnp.maximum(m_prev, jnp.max(s_ref[:, sl], axis=0, keepdims=True))
            a_ref[:, sl] = jnp.exp2(m_prev - m_new)
            p_ref[:, sl] = jnp.exp2(s_ref[:, sl] - m_new).astype(BF16)
            m_scr[:, sl] = m_new

    def pv(c, p_ref, a_ref):
        off = pl.multiple_of(c * tk, tk)
        acc_scr[...] = acc_scr[...] * a_ref[...] + jnp.dot(v_ref[:, pl.ds(off, tk)], p_ref[...],
                                                           preferred_element_type=F32)

    m_scr[...] = jnp.full(m_scr.shape, NEG, F32)
    acc_scr[...] = jnp.zeros(acc_scr.shape, F32)
    qk(0, s0)
    qk(1, s1)
    sm(s0, p0, a0)

    def body(j, carry):
        qk(2 * j, s0)
        sm(s1, p1, a1)
        pv(2 * j - 2, p0, a0)
        qk(2 * j + 1, s1)
        sm(s0, p0, a0)
        pv(2 * j - 1, p1, a1)
        return carry

    lax.fori_loop(1, nk // 2, body, 0)
    sm(s1, p1, a1)
    pv(nk - 2, p0, a0)
    pv(nk - 1, p1, a1)
    acc = acc_scr[...]
    o_ref[...] = (acc[:V_DIM] / acc[V_DIM:V_DIM + 1]).T.astype(o_ref.dtype)


def _mla_attn(qt, kp, vt, *, batch, length, tok_off, heads):
    tq = _pick(length, MLA_TQ_CANDS)
    tk = _pick(length // 2, MLA_TK_CANDS)
    nk = length // tk
    assert nk % 2 == 0 and tok_off % length == 0
    nq = length // tq
    qoff = tok_off // tq
    boff = tok_off // length
    return pl.pallas_call(
        functools.partial(_mla_attn_kernel, tk=tk, nk=nk),
        grid=(batch, heads, nq),
        in_specs=[pl.BlockSpec((HEAD_PAD, tq), lambda b, h, i: (h, qoff + b * nq + i)),
                  pl.BlockSpec((length, HEAD_PAD), lambda b, h, i: (boff + b, h)),
                  pl.BlockSpec((V_PAD, length), lambda b, h, i: (h, boff + b))],
        out_specs=pl.BlockSpec((tq, V_DIM), lambda b, h, i: (b * nq + i, h)),
        out_shape=jax.ShapeDtypeStruct((batch * length, heads * V_DIM), BF16),
        scratch_shapes=[pltpu.VMEM((tk, tq), F32), pltpu.VMEM((tk, tq), F32),
                        pltpu.VMEM((tk, tq), BF16), pltpu.VMEM((tk, tq), BF16),
                        pltpu.VMEM((1, tq), F32), pltpu.VMEM((1, tq), F32),
                        pltpu.VMEM((1, tq), F32), pltpu.VMEM((V_PAD, tq), F32)],
        compiler_params=_cparams("parallel", "parallel", "arbitrary"),
        name="mla_attn",
    )(qt, kp, vt)


def _na_kernel(q_ref, k0, k1, k2, k3, v0, v1, v2, v3, b_ref, o_ref, s_scr, p_scr, den_scr):
    pw = NA_PIECE
    qt = q_ref[...].T
    for j, kr in enumerate((k0, k1, k2, k3)):
        s_scr[j * pw:(j + 1) * pw, :] = (jnp.dot(kr[...], qt, preferred_element_type=F32)
                                         + b_ref[j * pw:(j + 1) * pw, :])
    for t in range(q_ref.shape[0] // NA_SM_STRIP):
        sl = slice(t * NA_SM_STRIP, (t + 1) * NA_SM_STRIP)
        m = jnp.max(s_scr[:, sl], axis=0, keepdims=True)
        p = jnp.exp2(s_scr[:, sl] - m)
        den_scr[:, sl] = jnp.sum(p, axis=0, keepdims=True)
        p_scr[:, sl] = p.astype(BF16)
    acc = None
    for j, vr in enumerate((v0, v1, v2, v3)):
        a = jnp.dot(vr[...].T, p_scr[j * pw:(j + 1) * pw, :], preferred_element_type=F32)
        acc = a if acc is None else acc + a
    o_ref[...] = (acc / den_scr[...]).T.astype(o_ref.dtype)


def _na_bias(rpb):
    H = rpb.shape[0]
    a = jnp.arange(NA_QROWS)
    j = jnp.arange(2 * NA_QROWS)
    c = jnp.arange(GRID_W)
    q_rel = jnp.stack([a, a + NA_KH // 2, a + NA_QROWS])
    rs_rel = jnp.stack([jnp.maximum(a - NA_KH // 2, 0), a, jnp.minimum(a + NA_KH // 2, NA_QROWS)])
    vrow = (j[None, None, :] >= rs_rel[:, :, None]) & (j[None, None, :] < rs_rel[:, :, None] + NA_KH)
    dy = jnp.clip(j[None, None, :] - q_rel[:, :, None] + NA_KH - 1, 0, 2 * NA_KH - 2)
    cs = jnp.clip(c - NA_KW // 2, 0, GRID_W - NA_KW)
    vcol = (c[None, :] >= cs[:, None]) & (c[None, :] < cs[:, None] + NA_KW)
    dx = jnp.clip(c[None, :] - c[:, None] + NA_KW - 1, 0, 2 * NA_KW - 2)
    oh_x = (dx[:, :, None] == jnp.arange(2 * NA_KW - 1)[None, None, :]).astype(F32)
    oh_y = (dy[..., None] == jnp.arange(2 * NA_KH - 1)[None, None, None, :]).astype(F32)
    b2 = jnp.einsum("hyx,cdx->hycd", rpb.astype(F32), oh_x, precision=lax.Precision.HIGHEST)
    b = jnp.einsum("kajy,hycd->hkacjd", oh_y, b2, precision=lax.Precision.HIGHEST)
    valid = vrow[:, :, None, :, None] & vcol[None, None, :, None, :]
    b = jnp.where(valid[None], b * LOG2E, NEG)
    return b.transpose(1, 0, 4, 5, 2, 3).reshape(3, H, NA_WTOK, NA_QTOK)


def _na_attn(z, bias, *, groups, heads, k_col, v_col):
    T = z.shape[0]
    (bp, lp), (bs, ls) = groups
    n_p = bp * lp // NA_QTOK
    nbp = lp // NA_QTOK
    nbs = ls // NA_QTOK
    assert nbp >= 2 and nbs >= 2, "needs at least 16 grid rows per sequence"
    piece_per_blk = NA_QTOK // NA_PIECE

    def meta(g):
        is_p = g < n_p
        gl = jnp.where(is_p, g, g - n_p)
        nb = jnp.where(is_p, nbp, nbs)
        seq = gl // nb
        i = gl % nb
        base = jnp.where(is_p, 0, bp * lp // NA_PIECE) + seq * (piece_per_blk * nb)
        w0 = jnp.clip(piece_per_blk * i - 1, 0, piece_per_blk * nb - 4)
        kind = jnp.where(i == 0, 0, jnp.where(i == nb - 1, 2, 1))
        return base + w0, kind

    def piece_spec(col, jj):
        return pl.BlockSpec((NA_PIECE, NA_DH), lambda h, g: (meta(g)[0] + jj, col + h))

    return pl.pallas_call(
        _na_kernel,
        grid=(heads, T // NA_QTOK),
        in_specs=([pl.BlockSpec((NA_QTOK, NA_DH), lambda h, g: (g, h))]
                  + [piece_spec(k_col // NA_DH, jj) for jj in range(4)]
                  + [piece_spec(v_col // NA_DH, jj) for jj in range(4)]
                  + [pl.BlockSpec((None, None, NA_WTOK, NA_QTOK), lambda h, g: (meta(g)[1], h, 0, 0))]),
        out_specs=pl.BlockSpec((NA_QTOK, NA_DH), lambda h, g: (g, h)),
        out_shape=jax.ShapeDtypeStruct((T, heads * NA_DH), BF16),
        scratch_shapes=[pltpu.VMEM((NA_WTOK, NA_QTOK), F32), pltpu.VMEM((NA_WTOK, NA_QTOK), BF16),
                        pltpu.VMEM((1, NA_QTOK), F32)],
        compiler_params=_cparams("parallel", "parallel"),
        name="na_attn",
    )(z, *([z] * 8), bias)


def _merge_kernel(ya_ref, yb0_ref, yb1_ref, ga_ref, gb_ref, x0_ref, x1in_ref, wa_ref, wb_ref, wo_ref, gm_ref,
                  wr_ref, br_ref, x1_ref, h2_ref, ridx_ref, rgate_ref, *, n_first):
    in_first = pl.program_id(0) < n_first
    ya = jnp.dot(ya_ref[...], wa_ref[...], preferred_element_type=F32)
    yb = jnp.dot(jnp.where(in_first, yb0_ref[...], yb1_ref[...]), wb_ref[...], preferred_element_type=F32)
    u = _sigmoid(ga_ref[...].astype(F32)) * ya + _sigmoid(gb_ref[...].astype(F32)) * yb
    x = jnp.where(in_first, x0_ref[...], x1in_ref[...])
    x1 = x + jnp.dot(u.astype(BF16), wo_ref[...], preferred_element_type=F32)
    x1_ref[...] = x1
    h2 = _rms(x1, gm_ref[...])
    _store_packed_rows(h2_ref, h2)
    logits = jnp.dot(h2.astype(BF16), wr_ref[...], preferred_element_type=F32) + br_ref[...]
    lane = lax.broadcasted_iota(I32, logits.shape, 1)
    vals, idxs = [], []
    for _ in range(TOP_K):
        mx = jnp.max(logits, axis=-1, keepdims=True)
        ix = jnp.min(jnp.where(logits == mx, lane, LANE), axis=-1, keepdims=True)
        vals.append(mx)
        idxs.append(ix)
        logits = jnp.where(lane == ix, 2 * NEG, logits)
    es = [jnp.exp(v - vals[0]) for v in vals]
    den = functools.reduce(lambda a, b: a + b, es)
    ridx = jnp.zeros(lane.shape, I32)
    rgate = jnp.zeros(lane.shape, F32)
    for k in range(TOP_K):
        ridx = jnp.where(lane == k, idxs[k], ridx)
        rgate = jnp.where(lane == k, es[k] / den, rgate)
    ridx_ref[...] = ridx
    rgate_ref[...] = rgate


def _merge(ya, yb0, yb1, z, x0, x1, wa, wb, wo, gm, wr, br, *, ga_col):
    D = x0.shape[1]
    T = x0.shape[0] + x1.shape[0]
    tm = 256
    n_first = x0.shape[0] // tm
    return pl.pallas_call(
        functools.partial(_merge_kernel, n_first=n_first),
        grid=(T // tm,),
        in_specs=[pl.BlockSpec((tm, D), lambda i: (i, 0))]
                 + _group_specs((tm, D), n_first)
                 + [pl.BlockSpec((tm, D), lambda i: (i, ga_col // D)),
                    pl.BlockSpec((tm, D), lambda i: (i, ga_col // D + 1))]
                 + _group_specs((tm, D), n_first)
                 + [_resident(wa.shape), _resident(wb.shape), _resident(wo.shape),
                    _resident((1, D)), _resident((D, LANE)), _resident((1, LANE))],
        out_specs=[pl.BlockSpec((tm, D), lambda i: (i, 0)),
                   pl.BlockSpec((tm * SUBLANE, LANE), lambda i: (i, 0)),
                   pl.BlockSpec((tm, LANE), lambda i: (i, 0)),
                   pl.BlockSpec((tm, LANE), lambda i: (i, 0))],
        out_shape=[jax.ShapeDtypeStruct((T, D), F32), jax.ShapeDtypeStruct((T * SUBLANE, LANE), U32),
                   jax.ShapeDtypeStruct((T, LANE), I32), jax.ShapeDtypeStruct((T, LANE), F32)],
        compiler_params=_cparams("parallel"),
        name="merge_router",
    )(ya, yb0, yb1, z, z, x0, x1, wa, wb, wo, gm, wr, br)


def _row_tile(ref, r):
    return ref.at[pl.ds(pl.multiple_of(r * SUBLANE, SUBLANE), SUBLANE), :]


def _dispatch_kernel(dest_ref, h_ref, xs_in_ref, xs_ref, sem, *, tm):
    del xs_in_ref

    def body(r, c):
        for k in range(TOP_K):
            pltpu.make_async_copy(_row_tile(h_ref, r), _row_tile(xs_ref, dest_ref[TOP_K * r + k]), sem).start()
        return c

    lax.fori_loop(0, tm, body, 0)
    for _ in range(TOP_K):
        pltpu.make_async_copy(h_ref, xs_ref.at[pl.ds(0, tm * SUBLANE), :], sem).wait()


def _dispatch(dest, h2p, n_rows):
    T = h2p.shape[0] // SUBLANE
    tm = 256
    xs0 = jnp.zeros((n_rows * SUBLANE, LANE), U32)
    return pl.pallas_call(
        functools.partial(_dispatch_kernel, tm=tm),
        grid=(T // tm,),
        in_specs=[pl.BlockSpec((TOP_K * tm,), lambda i: (i,), memory_space=pltpu.SMEM),
                  pl.BlockSpec((tm * SUBLANE, LANE), lambda i: (i, 0)),
                  pl.BlockSpec(memory_space=pl.ANY)],
        out_specs=pl.BlockSpec(memory_space=pl.ANY),
        out_shape=jax.ShapeDtypeStruct((n_rows * SUBLANE, LANE), U32),
        scratch_shapes=[pltpu.SemaphoreType.DMA],
        input_output_aliases={2: 0},
        compiler_params=_cparams("arbitrary"),
        name="moe_dispatch",
    )(dest, h2p, xs0)


def _expert_kernel(be_ref, nu_ref, xs_ref, wg_ref, bg_ref, wu_ref, bu_ref, wd_ref, bd_ref, ys_ref,
                   xb_scr, acc_scr, *, nf):
    i = pl.program_id(0)
    f = pl.program_id(1)
    bm = xb_scr.shape[0]
    used = i < nu_ref[0]

    @pl.when(used)
    def _():
        @pl.when(f == 0)
        def _():
            for col, blk in _load_packed_rows(xs_ref, bm):
                xb_scr[:, col:col + LANE] = blk.astype(BF16)

        xb = xb_scr[...]
        g = jnp.dot(xb, wg_ref[0], preferred_element_type=F32) + bg_ref[0]
        u = jnp.dot(xb, wu_ref[0], preferred_element_type=F32) + bu_ref[0]
        g = jnp.minimum(g, SWIGLU_LIMIT)
        u = jnp.clip(u, -SWIGLU_LIMIT, SWIGLU_LIMIT)
        a = g * _sigmoid(SWIGLU_ALPHA * g) * (u + 1.0)
        part = jnp.dot(a.astype(BF16), wd_ref[0], preferred_element_type=F32)

        @pl.when(f == 0)
        def _():
            acc_scr[...] = part + bd_ref[0]

        @pl.when(f > 0)
        def _():
            acc_scr[...] += part

    @pl.when(f == nf - 1)
    def _():
        @pl.when(used)
        def _():
            _store_packed_rows(ys_ref, acc_scr[...])

        @pl.when(jnp.logical_not(used))
        def _():
            ys_ref[...] = jnp.zeros(ys_ref.shape, U32)


def _experts(block_e, n_used, xs, wg, bg, wu, bu, wd, bd):
    P = xs.shape[0] // SUBLANE
    E, D, F = wg.shape
    tf = _pick(F, (1024, 512, 256, 128))
    nf = F // tf
    nblk = P // MOE_BM

    def blk(i, nu):
        return jnp.minimum(i, nu[0] - 1)

    def fidx(i, f, nu):
        return jnp.where(i < nu[0], f, nf - 1)

    grid_spec = pltpu.PrefetchScalarGridSpec(
        num_scalar_prefetch=2,
        grid=(nblk, nf),
        in_specs=[pl.BlockSpec((MOE_BM * SUBLANE, LANE), lambda i, f, be, nu: (blk(i, nu), 0)),
                  pl.BlockSpec((1, D, tf), lambda i, f, be, nu: (be[blk(i, nu)], 0, fidx(i, f, nu))),
                  pl.BlockSpec((1, 1, tf), lambda i, f, be, nu: (be[blk(i, nu)], 0, fidx(i, f, nu))),
                  pl.BlockSpec((1, D, tf), lambda i, f, be, nu: (be[blk(i, nu)], 0, fidx(i, f, nu))),
                  pl.BlockSpec((1, 1, tf), lambda i, f, be, nu: (be[blk(i, nu)], 0, fidx(i, f, nu))),
                  pl.BlockSpec((1, tf, D), lambda i, f, be, nu: (be[blk(i, nu)], fidx(i, f, nu), 0)),
                  pl.BlockSpec((1, 1, D), lambda i, f, be, nu: (be[blk(i, nu)], 0, 0))],
        out_specs=pl.BlockSpec((MOE_BM * SUBLANE, LANE), lambda i, f, be, nu: (i, 0)),
        scratch_shapes=[pltpu.VMEM((MOE_BM, D), BF16), pltpu.VMEM((MOE_BM, D), F32)],
    )
    return pl.pallas_call(
        functools.partial(_expert_kernel, nf=nf),
        grid_spec=grid_spec,
        out_shape=jax.ShapeDtypeStruct((P * SUBLANE, LANE), U32),
        compiler_params=_cparams("arbitrary", "arbitrary"),
        name="moe_experts",
    )(block_e, n_used, xs, wg, bg, wu, bu, wd, bd)


def _combine_kernel(pos_ref, ys_ref, x1_ref, gate_ref, p_ref, gp_ref, wpg_ref, wpp_ref, gf_ref, o_ref,
                    b0, b1, b2, b3, x2_scr, sem, *, tm, final):
    bufs = (b0, b1, b2, b3)

    def body(r, c):
        for k in range(TOP_K):
            pltpu.make_async_copy(_row_tile(ys_ref, pos_ref[TOP_K * r + k]), _row_tile(bufs[k], r), sem).start()
        return c

    lax.fori_loop(0, tm, body, 0)
    for k in range(TOP_K):
        pltpu.make_async_copy(ys_ref.at[pl.ds(0, tm * SUBLANE), :], bufs[k], sem).wait()

    gates = gate_ref[...]
    x2_scr[...] = x1_ref[...]
    for k in range(TOP_K):
        g = gates[:, k:k + 1]
        for col, blk in _load_packed_rows(bufs[k], tm):
            x2_scr[:, col:col + LANE] += g * blk
    x2 = x2_scr[...]
    h3 = _rms(x2, gp_ref[...]).astype(BF16)
    gt = _sigmoid(jnp.dot(h3, wpg_ref[...], preferred_element_type=F32))
    pe = jnp.dot(p_ref[...].astype(BF16), wpp_ref[...], preferred_element_type=F32)
    x3 = x2 + gt * pe
    o_ref[...] = _rms(x3, gf_ref[...]) if final else x3


def _combine(pos, ys, x1, gates, p, gp, wpg, wpp, gf, *, tok_off, n_tok, final):
    D = x1.shape[1]
    tm = 256
    off = tok_off // tm
    return pl.pallas_call(
        functools.partial(_combine_kernel, tm=tm, final=final),
        grid=(n_tok // tm,),
        in_specs=[pl.BlockSpec((TOP_K * tm,), lambda i: (off + i,), memory_space=pltpu.SMEM),
                  pl.BlockSpec(memory_space=pl.ANY),
                  pl.BlockSpec((tm, D), lambda i: (off + i, 0)),
                  pl.BlockSpec((tm, LANE), lambda i: (off + i, 0)),
                  pl.BlockSpec((tm, p.shape[1]), lambda i: (off + i, 0)),
                  _resident((1, D)), _resident(wpg.shape), _resident(wpp.shape), _resident((1, D))],
        out_specs=pl.BlockSpec((tm, D), lambda i: (i, 0)),
        out_shape=jax.ShapeDtypeStruct((n_tok, D), F32),
        scratch_shapes=[pltpu.VMEM((tm * SUBLANE, LANE), U32)] * TOP_K
                       + [pltpu.VMEM((tm, D), F32), pltpu.SemaphoreType.DMA],
        compiler_params=_cparams("arbitrary"),
        name="moe_combine_ple",
    )(pos, ys, x1, gates, p, gp, wpg, wpp, gf)


def _route(ridx, n_exp):
    flat_e = ridx.reshape(-1)
    tk = flat_e.shape[0]
    oh = (flat_e[:, None] == jnp.arange(n_exp, dtype=I32)[None, :]).astype(I32)
    csum = jnp.cumsum(oh, axis=0)
    rank = jnp.sum(oh * csum, axis=1) - 1
    counts = csum[-1]
    pcounts = (counts + MOE_BM - 1) // MOE_BM * MOE_BM
    pend = jnp.cumsum(pcounts)
    pstart = pend - pcounts
    dest = (pstart[flat_e] + rank).astype(I32)
    nblk = (tk + n_exp * (MOE_BM - 1) + MOE_BM - 1) // MOE_BM
    blk_start = jnp.arange(nblk, dtype=I32) * MOE_BM
    block_e = jnp.minimum(jnp.sum((blk_start[:, None] >= pend[None, :]).astype(I32), axis=1), n_exp - 1)
    n_used = (pend[-1:] // MOE_BM).astype(I32)
    return dest, block_e, n_used, nblk * MOE_BM


def _swap_halves(w):
    half = w.shape[-1] // 2
    return jnp.concatenate([w[..., half:], w[..., :half]], axis=-1)


def _rope_tables(length):
    pos = jnp.arange(length, dtype=F32)
    inv_freq = ROPE_THETA ** (-jnp.arange(0, ROPE, 2, dtype=F32) / ROPE)
    ang = pos[:, None] * inv_freq[None, :]
    cos, sin = jnp.cos(ang), jnp.sin(ang)
    cos2 = jnp.concatenate([cos, cos], axis=-1)
    sin2 = jnp.concatenate([-sin, sin], axis=-1)
    zeros = jnp.zeros((length, LANE - ROPE), F32)
    return (jnp.concatenate([cos2, zeros], axis=-1), jnp.concatenate([sin2, zeros], axis=-1),
            cos2.T, sin2.T)


def kernel(x_prompt, x_sample, p_prompt, p_sample, g_mix, w_in, g_q_lat, w_q_up, g_kv_lat, w_kv_up, na_rpb, w_br_a, w_br_b, w_out, g_moe, w_router, b_router, w_gate, b_gate, w_up, b_up, w_down, b_down, g_ple, w_ple_gate, w_ple_proj, g_final):
    bp, lp, D = x_prompt.shape
    bs, ls, _ = x_sample.shape
    tp, ts = bp * lp, bs * ls
    groups = ((bp, lp), (bs, ls))
    depth = w_in.shape[0]
    na_heads = na_rpb.shape[1]
    na_w = na_heads * NA_DH
    ql = g_q_lat.shape[-1]
    kvl = g_kv_lat.shape[-1]
    mla_heads = w_q_up.shape[-1] // (NOPE + ROPE)
    n_exp = w_router.shape[-1]
    assert na_w == D and ql == kvl and lp >= ls

    xs_in = (x_prompt.reshape(tp, D), x_sample.reshape(ts, D))
    rope_tabs = _rope_tables(lp)
    outs = None
    for li in range(depth):
        p = jnp.concatenate([p_prompt[li].reshape(tp, -1), p_sample[li].reshape(ts, -1)], axis=0)
        wi = w_in[li]
        o = [0]
        for wdt in (na_w, na_w, na_w, ql, kvl, ROPE, D, D):
            o.append(o[-1] + wdt)
        w_qa, w_ka, w_va, w_cq, w_ckv, w_kr, w_ga, w_gb = (wi[:, o[k]:o[k + 1]] for k in range(8))
        w_main = jnp.concatenate([w_qa * (NA_DH ** -0.5 * LOG2E), w_ka, w_va, w_ga, w_gb, w_cq, w_ckv],
                                 axis=1).astype(BF16)
        ga_col = 3 * na_w
        cq_col = 3 * na_w + 2 * D
        w_kr2 = jnp.concatenate([w_kr, _swap_halves(w_kr)], axis=1).astype(BF16)
        wq3 = w_q_up[li].reshape(ql, mla_heads, NOPE + ROPE) * ((NOPE + ROPE) ** -0.5 * LOG2E)
        wqt = jnp.concatenate([wq3[..., :NOPE], wq3[..., NOPE:], _swap_halves(wq3[..., NOPE:])],
                              axis=-1).reshape(ql, mla_heads * HEAD_PAD).T.astype(BF16)
        wkv3 = w_kv_up[li].reshape(kvl, mla_heads, NOPE + V_DIM)
        wk = wkv3[..., :NOPE].reshape(kvl, mla_heads * NOPE).astype(BF16)
        wvt = wkv3[..., NOPE:].reshape(kvl, mla_heads * V_DIM).T.astype(BF16)
        bias = _na_bias(na_rpb[li])
        wr = jnp.pad(w_router[li], ((0, 0), (0, LANE - n_exp))).astype(BF16)
        br = jnp.pad(b_router[li].astype(F32), (0, LANE - n_exp), constant_values=NEG)[None, :]

        z, kr = _inproj(xs_in[0], xs_in[1], g_mix[li][None, :], w_main, w_kr2)
        qt, kp, vt = _mlaproj(z, kr, rope_tabs, g_q_lat[li][None, :], g_kv_lat[li][None, :], wqt, wk, wvt,
                              cq_col=cq_col, groups=groups, heads=mla_heads)
        ya = _na_attn(z, bias, groups=groups, heads=na_heads, k_col=na_w, v_col=2 * na_w)
        yb0 = _mla_attn(qt, kp, vt, batch=bp, length=lp, tok_off=0, heads=mla_heads)
        yb1 = _mla_attn(qt, kp, vt, batch=bs, length=ls, tok_off=tp, heads=mla_heads)
        x1, h2, ridx, rgate = _merge(ya, yb0, yb1, z, xs_in[0], xs_in[1], w_br_a[li].astype(BF16),
                                     w_br_b[li].astype(BF16), w_out[li].astype(BF16), g_moe[li][None, :],
                                     wr, br, ga_col=ga_col)

        dest, block_e, n_used, n_rows = _route(ridx[:, :TOP_K], n_exp)
        xs = _dispatch(dest, h2, n_rows)
        ys = _experts(block_e, n_used, xs,
                      w_gate[li].astype(BF16), b_gate[li][:, None, :], w_up[li].astype(BF16),
                      b_up[li][:, None, :], w_down[li].astype(BF16), b_down[li][:, None, :])
        final = li == depth - 1
        comb = functools.partial(_combine, dest, ys, x1, rgate, p, g_ple[li][None, :],
                                 w_ple_gate[li].astype(BF16), w_ple_proj[li].astype(BF16),
                                 g_final[None, :], final=final)
        outs = (comb(tok_off=0, n_tok=tp), comb(tok_off=tp, n_tok=ts))
        xs_in = outs
    return (outs[0].reshape(bp, lp, D), outs[1].reshape(bs, ls, D))
```

```python
import functools

import jax
import jax.numpy as jnp
from jax import lax
from jax.experimental import pallas as pl
from jax.experimental.pallas import tpu as pltpu

F32 = jnp.float32
BF16 = jnp.bfloat16
I32 = jnp.int32
U32 = jnp.uint32

LANE = 128
SUBLANE = 8
ROW_WORDS = SUBLANE * LANE
VMEM_LIMIT_BYTES = 56 * 1024 * 1024

RMS_EPS = 1e-6
GRID_W = 64
NA_KH = 8
NA_KW = 16
NA_DH = 128
NA_QROWS = 8
NA_QTOK = NA_QROWS * GRID_W
NA_WTOK = 2 * NA_QTOK
NA_PIECE = NA_WTOK // 4
NA_SM_STRIP = 128
NA_HEADS_PER_STEP = 8
NOPE = 128
ROPE = 64
V_DIM = 128
HEAD_PAD = 256
V_PAD = 144
ROPE_THETA = 10000.0
TOP_K = 4
SWIGLU_LIMIT = 7.0
SWIGLU_ALPHA = 1.702
MOE_BM = 512
MLA_TQ_CANDS = (2048, 1024, 512)
MLA_TK_CANDS = (512, 256)
MLA_SM_STRIP = 256
LOG2E = 1.4426950408889634
NEG = -1e30


def _cparams(*sem):
    return pltpu.CompilerParams(dimension_semantics=sem, vmem_limit_bytes=VMEM_LIMIT_BYTES)


def _pick(n, cands):
    for c in cands:
        if n % c == 0:
            return c
    raise ValueError(f"no tile in {cands} divides {n}")


def _resident(shape):
    nd = len(shape)
    return pl.BlockSpec(shape, lambda *_: (0,) * nd, pipeline_mode=pl.Buffered(1))


def _rms(x, g):
    return x * lax.rsqrt(jnp.mean(x * x, axis=-1, keepdims=True) + RMS_EPS) * g


def _sigmoid(x):
    return 1.0 / (1.0 + jnp.exp(-x))


def _pack_pair(lo, hi):
    lo_bits = lax.bitcast_convert_type(lo.astype(BF16).astype(F32), U32) >> 16
    hi_bits = lax.bitcast_convert_type(hi.astype(BF16).astype(F32), U32)
    return lo_bits | hi_bits


def _unpack_pair(w):
    lo = lax.bitcast_convert_type(w << 16, F32)
    hi = lax.bitcast_convert_type(w & jnp.uint32(0xFFFF0000), F32)
    return lo, hi


def _store_packed_rows(ref, x):
    rows, d = x.shape
    assert d == 2 * ROW_WORDS
    for i in range(SUBLANE):
        lo = x[:, i * LANE:(i + 1) * LANE]
        hi = x[:, d // 2 + i * LANE:d // 2 + (i + 1) * LANE]
        ref[pl.ds(i, rows, stride=SUBLANE), :] = _pack_pair(lo, hi)


def _load_packed_rows(ref, rows):
    half = ROW_WORDS
    for i in range(SUBLANE):
        lo, hi = _unpack_pair(ref[pl.ds(i, rows, stride=SUBLANE), :])
        yield i * LANE, lo
        yield half + i * LANE, hi


def _group_specs(block, n_first, **kw):
    first = pl.BlockSpec(block, lambda i, *_: (jnp.minimum(i, n_first - 1), 0), **kw)
    second = pl.BlockSpec(block, lambda i, *_: (jnp.maximum(i - n_first, 0), 0), **kw)
    return [first, second]


def _inproj_kernel(xa_ref, xb_ref, g_ref, w_ref, wkr_ref, z_ref, kr_ref, h_scr, *, n_first):
    @pl.when(pl.program_id(1) == 0)
    def _():
        x = jnp.where(pl.program_id(0) < n_first, xa_ref[...], xb_ref[...])
        hb = _rms(x, g_ref[...]).astype(BF16)
        h_scr[...] = hb
        kr_ref[...] = jnp.dot(hb, wkr_ref[...], preferred_element_type=F32)

    z_ref[...] = jnp.dot(h_scr[...], w_ref[...], preferred_element_type=F32).astype(z_ref.dtype)


def _inproj(xa, xb, g, w, wkr):
    D = xa.shape[1]
    T = xa.shape[0] + xb.shape[0]
    N = w.shape[1]
    tm = _pick(xb.shape[0], (1024, 512, 256))
    assert xa.shape[0] % tm == 0
    tn = _pick(N, (1024, 512, 256, 128))
    n_first = xa.shape[0] // tm
    return pl.pallas_call(
        functools.partial(_inproj_kernel, n_first=n_first),
        grid=(T // tm, N // tn),
        in_specs=_group_specs((tm, D), n_first, pipeline_mode=pl.Buffered(1)) + [
                  pl.BlockSpec((1, D), lambda i, j: (0, 0)),
                  pl.BlockSpec((D, tn), lambda i, j: (0, j)),
                  pl.BlockSpec((D, LANE), lambda i, j: (0, 0))],
        out_specs=[pl.BlockSpec((tm, tn), lambda i, j: (i, j)),
                   pl.BlockSpec((tm, LANE), lambda i, j: (i, 0))],
        out_shape=[jax.ShapeDtypeStruct((T, N), BF16), jax.ShapeDtypeStruct((T, LANE), F32)],
        scratch_shapes=[pltpu.VMEM((tm, D), BF16)],
        compiler_params=_cparams("parallel", "arbitrary"),
        name="inproj",
    )(xa, xb, g, w, wkr)


def _mlaproj_kernel(cq_ref, ckv_ref, kr_ref, cos_ref, sin_ref, cost_ref, sint_ref, gq_ref, gkv_ref,
                    wqt_ref, wk_ref, wvt_ref, qt_out, k_out, vt_out, *, heads):
    tm = cq_ref.shape[0]
    cqn = _rms(cq_ref[...].astype(F32), gq_ref[...]).astype(BF16)
    ckvn = _rms(ckv_ref[...].astype(F32), gkv_ref[...]).astype(BF16)
    ckvn_t = ckvn.T
    qt = jnp.dot(wqt_ref[...], cqn.T, preferred_element_type=F32)
    kn = jnp.dot(ckvn, wk_ref[...], preferred_element_type=F32)
    vt = jnp.dot(wvt_ref[...], ckvn_t, preferred_element_type=F32)
    kr = kr_ref[...]
    krope = (kr * cos_ref[...] + pltpu.roll(kr, ROPE, axis=1) * sin_ref[...]).astype(BF16)
    cos_t = cost_ref[...]
    sin_t = sint_ref[...]
    tail = (lax.broadcasted_iota(I32, (V_PAD - V_DIM, tm), 0) == 0).astype(BF16)
    for h in range(heads):
        c = h * HEAD_PAD
        qt_out[c:c + NOPE, :] = qt[c:c + NOPE].astype(BF16)
        x = qt[c + NOPE:c + NOPE + ROPE]
        x_swapped = qt[c + NOPE + ROPE:c + HEAD_PAD]
        qt_out[c + NOPE:c + NOPE + ROPE, :] = (x * cos_t + x_swapped * sin_t).astype(BF16)
        qt_out[c + NOPE + ROPE:c + HEAD_PAD, :] = jnp.zeros((ROPE, tm), BF16)
        k_out[:, c:c + NOPE] = kn[:, h * NOPE:(h + 1) * NOPE].astype(BF16)
        k_out[:, c + NOPE:c + HEAD_PAD] = krope
        vt_out[h * V_PAD:h * V_PAD + V_DIM, :] = vt[h * V_DIM:(h + 1) * V_DIM].astype(BF16)
        vt_out[h * V_PAD + V_DIM:(h + 1) * V_PAD, :] = tail


def _mlaproj(z, kr, tabs, gq, gkv, wqt, wk, wvt, *, cq_col, groups, heads):
    T = z.shape[0]
    ql = gq.shape[-1]
    kvl = gkv.shape[-1]
    W = heads * HEAD_PAD
    (bp, lp), (bs, ls) = groups
    tm = _pick(ls, (256, 128))
    n_p = bp * lp // tm
    cos_n, sin_n, cos_t, sin_t = tabs

    def pos_blk(i):
        return jnp.where(i < n_p, i % (lp // tm), (i - n_p) % (ls // tm))

    return pl.pallas_call(
        functools.partial(_mlaproj_kernel, heads=heads),
        grid=(T // tm,),
        in_specs=[pl.BlockSpec((tm, ql), lambda i: (i, cq_col // ql)),
                  pl.BlockSpec((tm, kvl), lambda i: (i, (cq_col + ql) // kvl)),
                  pl.BlockSpec((tm, LANE), lambda i: (i, 0)),
                  pl.BlockSpec((tm, LANE), lambda i: (pos_blk(i), 0)),
                  pl.BlockSpec((tm, LANE), lambda i: (pos_blk(i), 0)),
                  pl.BlockSpec((ROPE, tm), lambda i: (0, pos_blk(i))),
                  pl.BlockSpec((ROPE, tm), lambda i: (0, pos_blk(i))),
                  _resident((1, ql)), _resident((1, kvl)),
                  _resident(wqt.shape), _resident(wk.shape), _resident(wvt.shape)],
        out_specs=[pl.BlockSpec((W, tm), lambda i: (0, i)),
                   pl.BlockSpec((tm, W), lambda i: (i, 0)),
                   pl.BlockSpec((heads * V_PAD, tm), lambda i: (0, i))],
        out_shape=[jax.ShapeDtypeStruct((W, T), BF16), jax.ShapeDtypeStruct((T, W), BF16),
                   jax.ShapeDtypeStruct((heads * V_PAD, T), BF16)],
        compiler_params=_cparams("parallel"),
        name="mlaproj",
    )(z, z, kr, cos_n, sin_n, cos_t, sin_t, gq, gkv, wqt, wk, wvt)


def _mla_attn_kernel(q_ref, k_ref, v_ref, o_ref, s0, s1, p0, p1, a0, a1, m_scr, acc_scr, *, tk, nk):
    qt = q_ref[...]

    def qk(c, s_ref):
        off = pl.multiple_of(c * tk, tk)
        s_ref[...] = jnp.dot(k_ref[pl.ds(off, tk), :], qt, preferred_element_type=F32)

    def sm(s_ref, p_ref, a_ref):
        for j in range(s_ref.shape[1] // MLA_SM_STRIP):
            sl = slice(j * MLA_SM_STRIP, (j + 1) * MLA_SM_STRIP)
            m_prev = m_scr[:, sl]
            m_new = jnp.maximum(m_prev, jnp.max(s_ref[:, sl], axis=0, keepdims=True))
            a_ref[:, sl] = jnp.exp2(m_prev - m_new)
            p_ref[:, sl] = jnp.exp2(s_ref[:, sl] - m_new).astype(BF16)
            m_scr[:, sl] = m_new

    def pv(c, p_ref, a_ref):
        off = pl.multiple_of(c * tk, tk)
        acc_scr[...] = acc_scr[...] * a_ref[...] + jnp.dot(v_ref[:, pl.ds(off, tk)], p_ref[...],
                                                           preferred_element_type=F32)

    m_scr[...] = jnp.full(m_scr.shape, NEG, F32)
    acc_scr[...] = jnp.zeros(acc_scr.shape, F32)
    qk(0, s0)
    qk(1, s1)
    sm(s0, p0, a0)

    def body(j, carry):
        qk(2 * j, s0)
        sm(s1, p1, a1)
        pv(2 * j - 2, p0, a0)
        qk(2 * j + 1, s1)
        sm(s0, p0, a0)
        pv(2 * j - 1, p1, a1)
        return carry

    lax.fori_loop(1, nk // 2, body, 0)
    sm(s1, p1, a1)
    pv(nk - 2, p0, a0)
    pv(nk - 1, p1, a1)
    acc = acc_scr[...]
    o_ref[...] = (acc[:V_DIM] / acc[V_DIM:V_DIM + 1]).T.astype(o_ref.dtype)


def _mla_attn(qt, kp, vt, *, batch, length, tok_off, heads):
    tq = _pick(length, MLA_TQ_CANDS)
    tk = _pick(length // 2, MLA_TK_CANDS)
    nk = length // tk
    assert nk % 2 == 0 and tok_off % length == 0
    nq = length // tq
    qoff = tok_off // tq
    boff = tok_off // length
    return pl.pallas_call(
        functools.partial(_mla_attn_kernel, tk=tk, nk=nk),
        grid=(batch, heads, nq),
        in_specs=[pl.BlockSpec((HEAD_PAD, tq), lambda b, h, i: (h, qoff + b * nq + i)),
                  pl.BlockSpec((length, HEAD_PAD), lambda b, h, i: (boff + b, h)),
                  pl.BlockSpec((V_PAD, length), lambda b, h, i: (h, boff + b))],
        out_specs=pl.BlockSpec((tq, V_DIM), lambda b, h, i: (b * nq + i, h)),
        out_shape=jax.ShapeDtypeStruct((batch * length, heads * V_DIM), BF16),
        scratch_shapes=[pltpu.VMEM((tk, tq), F32), pltpu.VMEM((tk, tq), F32),
                        pltpu.VMEM((tk, tq), BF16), pltpu.VMEM((tk, tq), BF16),
                        pltpu.VMEM((1, tq), F32), pltpu.VMEM((1, tq), F32),
                        pltpu.VMEM((1, tq), F32), pltpu.VMEM((V_PAD, tq), F32)],
        compiler_params=_cparams("parallel", "parallel", "arbitrary"),
        name="mla_attn",
    )(qt, kp, vt)


def _na_kernel(q_ref, k0, k1, k2, k3, v0, v1, v2, v3, b_ref, o_ref, s_scr, p_scr, den_scr):
    pw = NA_PIECE
    for hh in range(NA_HEADS_PER_STEP):
        cols = slice(hh * NA_DH, (hh + 1) * NA_DH)
        qt = q_ref[:, cols].T
        for j, kr in enumerate((k0, k1, k2, k3)):
            s_scr[j * pw:(j + 1) * pw, :] = (jnp.dot(kr[:, cols], qt, preferred_element_type=F32)
                                             + b_ref[hh, j * pw:(j + 1) * pw, :])
        for t in range(q_ref.shape[0] // NA_SM_STRIP):
            sl = slice(t * NA_SM_STRIP, (t + 1) * NA_SM_STRIP)
            m = jnp.max(s_scr[:, sl], axis=0, keepdims=True)
            p = jnp.exp2(s_scr[:, sl] - m)
            den_scr[:, sl] = jnp.sum(p, axis=0, keepdims=True)
            p_scr[:, sl] = p.astype(BF16)
        acc = None
        for j, vr in enumerate((v0, v1, v2, v3)):
            a = jnp.dot(vr[:, cols].T, p_scr[j * pw:(j + 1) * pw, :], preferred_element_type=F32)
            acc = a if acc is None else acc + a
        o_ref[:, cols] = (acc / den_scr[...]).T.astype(o_ref.dtype)


def _na_bias(rpb):
    H = rpb.shape[0]
    a = jnp.arange(NA_QROWS)
    j = jnp.arange(2 * NA_QROWS)
    c = jnp.arange(GRID_W)
    q_rel = jnp.stack([a, a + NA_KH // 2, a + NA_QROWS])
    rs_rel = jnp.stack([jnp.maximum(a - NA_KH // 2, 0), a, jnp.minimum(a + NA_KH // 2, NA_QROWS)])
    vrow = (j[None, None, :] >= rs_rel[:, :, None]) & (j[None, None, :] < rs_rel[:, :, None] + NA_KH)
    dy = jnp.clip(j[None, None, :] - q_rel[:, :, None] + NA_KH - 1, 0, 2 * NA_KH - 2)
    cs = jnp.clip(c - NA_KW // 2, 0, GRID_W - NA_KW)
    vcol = (c[None, :] >= cs[:, None]) & (c[None, :] < cs[:, None] + NA_KW)
    dx = jnp.clip(c[None, :] - c[:, None] + NA_KW - 1, 0, 2 * NA_KW - 2)
    oh_x = (dx[:, :, None] == jnp.arange(2 * NA_KW - 1)[None, None, :]).astype(F32)
    oh_y = (dy[..., None] == jnp.arange(2 * NA_KH - 1)[None, None, None, :]).astype(F32)
    b2 = jnp.einsum("hyx,cdx->hycd", rpb.astype(F32), oh_x, precision=lax.Precision.HIGHEST)
    b = jnp.einsum("kajy,hycd->hkacjd", oh_y, b2, precision=lax.Precision.HIGHEST)
    valid = vrow[:, :, None, :, None] & vcol[None, None, :, None, :]
    b = jnp.where(valid[None], b * LOG2E, NEG)
    return b.transpose(1, 0, 4, 5, 2, 3).reshape(3, H, NA_WTOK, NA_QTOK)


def _na_attn(z, bias, *, groups, heads, k_col, v_col):
    T = z.shape[0]
    (bp, lp), (bs, ls) = groups
    n_p = bp * lp // NA_QTOK
    nbp = lp // NA_QTOK
    nbs = ls // NA_QTOK
    assert nbp >= 2 and nbs >= 2, "needs at least 16 grid rows per sequence"
    piece_per_blk = NA_QTOK // NA_PIECE

    def meta(g):
        is_p = g < n_p
        gl = jnp.where(is_p, g, g - n_p)
        nb = jnp.where(is_p, nbp, nbs)
        seq = gl // nb
        i = gl % nb
        base = jnp.where(is_p, 0, bp * lp // NA_PIECE) + seq * (piece_per_blk * nb)
        w0 = jnp.clip(piece_per_blk * i - 1, 0, piece_per_blk * nb - 4)
        kind = jnp.where(i == 0, 0, jnp.where(i == nb - 1, 2, 1))
        return base + w0, kind

    hb = NA_HEADS_PER_STEP
    wcols = hb * NA_DH
    assert heads % hb == 0 and k_col % wcols == 0 and v_col % wcols == 0

    def piece_spec(col, jj):
        return pl.BlockSpec((NA_PIECE, wcols), lambda h, g: (meta(g)[0] + jj, col + h))

    return pl.pallas_call(
        _na_kernel,
        grid=(heads // hb, T // NA_QTOK),
        in_specs=([pl.BlockSpec((NA_QTOK, wcols), lambda h, g: (g, h))]
                  + [piece_spec(k_col // wcols, jj) for jj in range(4)]
                  + [piece_spec(v_col // wcols, jj) for jj in range(4)]
                  + [pl.BlockSpec((None, hb, NA_WTOK, NA_QTOK), lambda h, g: (meta(g)[1], h, 0, 0))]),
        out_specs=pl.BlockSpec((NA_QTOK, wcols), lambda h, g: (g, h)),
        out_shape=jax.ShapeDtypeStruct((T, heads * NA_DH), BF16),
        scratch_shapes=[pltpu.VMEM((NA_WTOK, NA_QTOK), F32), pltpu.VMEM((NA_WTOK, NA_QTOK), BF16),
                        pltpu.VMEM((1, NA_QTOK), F32)],
        compiler_params=_cparams("parallel", "parallel"),
        name="na_attn",
    )(z, *([z] * 8), bias)


def _merge_kernel(ya_ref, yb0_ref, yb1_ref, ga_ref, gb_ref, x0_ref, x1in_ref, wa_ref, wb_ref, wo_ref, gm_ref,
                  wr_ref, br_ref, x1_ref, h2_ref, ridx_ref, rgate_ref, *, n_first):
    in_first = pl.program_id(0) < n_first
    ya = jnp.dot(ya_ref[...], wa_ref[...], preferred_element_type=F32)
    yb = jnp.dot(jnp.where(in_first, yb0_ref[...], yb1_ref[...]), wb_ref[...], preferred_element_type=F32)
    u = _sigmoid(ga_ref[...].astype(F32)) * ya + _sigmoid(gb_ref[...].astype(F32)) * yb
    x = jnp.where(in_first, x0_ref[...], x1in_ref[...])
    x1 = x + jnp.dot(u.astype(BF16), wo_ref[...], preferred_element_type=F32)
    x1_ref[...] = x1
    h2 = _rms(x1, gm_ref[...])
    _store_packed_rows(h2_ref, h2)
    logits = jnp.dot(h2.astype(BF16), wr_ref[...], preferred_element_type=F32) + br_ref[...]
    lane = lax.broadcasted_iota(I32, logits.shape, 1)
    vals, idxs = [], []
    for _ in range(TOP_K):
        mx = jnp.max(logits, axis=-1, keepdims=True)
        ix = jnp.min(jnp.where(logits == mx, lane, LANE), axis=-1, keepdims=True)
        vals.append(mx)
        idxs.append(ix)
        logits = jnp.where(lane == ix, 2 * NEG, logits)
    es = [jnp.exp(v - vals[0]) for v in vals]
    den = functools.reduce(lambda a, b: a + b, es)
    ridx = jnp.zeros(lane.shape, I32)
    rgate = jnp.zeros(lane.shape, F32)
    for k in range(TOP_K):
        ridx = jnp.where(lane == k, idxs[k], ridx)
        rgate = jnp.where(lane == k, es[k] / den, rgate)
    ridx_ref[...] = ridx
    rgate_ref[...] = rgate


def _merge(ya, yb0, yb1, z, x0, x1, wa, wb, wo, gm, wr, br, *, ga_col):
    D = x0.shape[1]
    T = x0.shape[0] + x1.shape[0]
    tm = 256
    n_first = x0.shape[0] // tm
    return pl.pallas_call(
        functools.partial(_merge_kernel, n_first=n_first),
        grid=(T // tm,),
        in_specs=[pl.BlockSpec((tm, D), lambda i: (i, 0))]
                 + _group_specs((tm, D), n_first)
                 + [pl.BlockSpec((tm, D), lambda i: (i, ga_col // D)),
                    pl.BlockSpec((tm, D), lambda i: (i, ga_col // D + 1))]
                 + _group_specs((tm, D), n_first)
                 + [_resident(wa.shape), _resident(wb.shape), _resident(wo.shape),
                    _resident((1, D)), _resident((D, LANE)), _resident((1, LANE))],
        out_specs=[pl.BlockSpec((tm, D), lambda i: (i, 0)),
                   pl.BlockSpec((tm * SUBLANE, LANE), lambda i: (i, 0)),
                   pl.BlockSpec((tm, LANE), lambda i: (i, 0)),
                   pl.BlockSpec((tm, LANE), lambda i: (i, 0))],
        out_shape=[jax.ShapeDtypeStruct((T, D), F32), jax.ShapeDtypeStruct((T * SUBLANE, LANE), U32),
                   jax.ShapeDtypeStruct((T, LANE), I32), jax.ShapeDtypeStruct((T, LANE), F32)],
        compiler_params=_cparams("parallel"),
        name="merge_router",
    )(ya, yb0, yb1, z, z, x0, x1, wa, wb, wo, gm, wr, br)


def _row_tile(ref, r):
    return ref.at[pl.ds(pl.multiple_of(r * SUBLANE, SUBLANE), SUBLANE), :]


def _dispatch_kernel(dest_ref, h_ref, xs_in_ref, xs_ref, sem, *, tm):
    del xs_in_ref

    def body(r, c):
        for k in range(TOP_K):
            pltpu.make_async_copy(_row_tile(h_ref, r), _row_tile(xs_ref, dest_ref[TOP_K * r + k]), sem).start()
        return c

    lax.fori_loop(0, tm, body, 0)
    for _ in range(TOP_K):
        pltpu.make_async_copy(h_ref, xs_ref.at[pl.ds(0, tm * SUBLANE), :], sem).wait()


def _dispatch(dest, h2p, n_rows):
    T = h2p.shape[0] // SUBLANE
    tm = 256
    xs0 = jnp.zeros((n_rows * SUBLANE, LANE), U32)
    return pl.pallas_call(
        functools.partial(_dispatch_kernel, tm=tm),
        grid=(T // tm,),
        in_specs=[pl.BlockSpec((TOP_K * tm,), lambda i: (i,), memory_space=pltpu.SMEM),
                  pl.BlockSpec((tm * SUBLANE, LANE), lambda i: (i, 0)),
                  pl.BlockSpec(memory_space=pl.ANY)],
        out_specs=pl.BlockSpec(memory_space=pl.ANY),
        out_shape=jax.ShapeDtypeStruct((n_rows * SUBLANE, LANE), U32),
        scratch_shapes=[pltpu.SemaphoreType.DMA],
        input_output_aliases={2: 0},
        compiler_params=_cparams("arbitrary"),
        name="moe_dispatch",
    )(dest, h2p, xs0)


def _expert_kernel(be_ref, nu_ref, xs_ref, wg_ref, bg_ref, wu_ref, bu_ref, wd_ref, bd_ref, ys_ref,
                   xb_scr, acc_scr, *, nf):
    i = pl.program_id(0)
    f = pl.program_id(1)
    bm = xb_scr.shape[0]
    used = i < nu_ref[0]

    @pl.when(used)
    def _():
        @pl.when(f == 0)
        def _():
            for col, blk in _load_packed_rows(xs_ref, bm):
                xb_scr[:, col:col + LANE] = blk.astype(BF16)
            acc_scr[...] = jnp.broadcast_to(bd_ref[0], acc_scr.shape)

        xb = xb_scr[...]
        g = jnp.dot(xb, wg_ref[0], preferred_element_type=F32) + bg_ref[0]
        u = jnp.dot(xb, wu_ref[0], preferred_element_type=F32) + bu_ref[0]
        g = jnp.minimum(g, SWIGLU_LIMIT)
        u = jnp.clip(u, -SWIGLU_LIMIT, SWIGLU_LIMIT)
        a = g * _sigmoid(SWIGLU_ALPHA * g) * (u + 1.0)
        acc_scr[...] += jnp.dot(a.astype(BF16), wd_ref[0], preferred_element_type=F32)

    @pl.when(f == nf - 1)
    def _():
        @pl.when(used)
        def _():
            _store_packed_rows(ys_ref, acc_scr[...])

        @pl.when(jnp.logical_not(used))
        def _():
            ys_ref[...] = jnp.zeros(ys_ref.shape, U32)


def _experts(block_e, n_used, xs, wg, bg, wu, bu, wd, bd):
    P = xs.shape[0] // SUBLANE
    E, D, F = wg.shape
    tf = _pick(F, (1024, 512, 256, 128))
    nf = F // tf
    nblk = P // MOE_BM

    def blk(i, nu):
        return jnp.minimum(i, nu[0] - 1)

    def fidx(i, f, nu):
        return jnp.where(i < nu[0], f, nf - 1)

    grid_spec = pltpu.PrefetchScalarGridSpec(
        num_scalar_prefetch=2,
        grid=(nblk, nf),
        in_specs=[pl.BlockSpec((MOE_BM * SUBLANE, LANE), lambda i, f, be, nu: (blk(i, nu), 0)),
                  pl.BlockSpec((1, D, tf), lambda i, f, be, nu: (be[blk(i, nu)], 0, fidx(i, f, nu))),
                  pl.BlockSpec((1, 1, tf), lambda i, f, be, nu: (be[blk(i, nu)], 0, fidx(i, f, nu))),
                  pl.BlockSpec((1, D, tf), lambda i, f, be, nu: (be[blk(i, nu)], 0, fidx(i, f, nu))),
                  pl.BlockSpec((1, 1, tf), lambda i, f, be, nu: (be[blk(i, nu)], 0, fidx(i, f, nu))),
                  pl.BlockSpec((1, tf, D), lambda i, f, be, nu: (be[blk(i, nu)], fidx(i, f, nu), 0)),
                  pl.BlockSpec((1, 1, D), lambda i, f, be, nu: (be[blk(i, nu)], 0, 0))],
        out_specs=pl.BlockSpec((MOE_BM * SUBLANE, LANE), lambda i, f, be, nu: (i, 0)),
        scratch_shapes=[pltpu.VMEM((MOE_BM, D), BF16), pltpu.VMEM((MOE_BM, D), F32)],
    )
    return pl.pallas_call(
        functools.partial(_expert_kernel, nf=nf),
        grid_spec=grid_spec,
        out_shape=jax.ShapeDtypeStruct((P * SUBLANE, LANE), U32),
        compiler_params=_cparams("arbitrary", "arbitrary"),
        name="moe_experts",
    )(block_e, n_used, xs, wg, bg, wu, bu, wd, bd)


def _combine_kernel(pos_ref, ys_ref, x1_ref, gate_ref, p_ref, gp_ref, wpg_ref, wpp_ref, gf_ref, o_ref,
                    b0, b1, b2, b3, x2_scr, sem, *, tm, final):
    bufs = (b0, b1, b2, b3)

    def body(r, c):
        for k in range(TOP_K):
            pltpu.make_async_copy(_row_tile(ys_ref, pos_ref[TOP_K * r + k]), _row_tile(bufs[k], r), sem).start()
        return c

    lax.fori_loop(0, tm, body, 0)
    for k in range(TOP_K):
        pltpu.make_async_copy(ys_ref.at[pl.ds(0, tm * SUBLANE), :], bufs[k], sem).wait()

    gates = gate_ref[...]
    x2_scr[...] = x1_ref[...]
    for k in range(TOP_K):
        g = gates[:, k:k + 1]
        for col, blk in _load_packed_rows(bufs[k], tm):
            x2_scr[:, col:col + LANE] += g * blk
    x2 = x2_scr[...]
    h3 = _rms(x2, gp_ref[...]).astype(BF16)
    gt = _sigmoid(jnp.dot(h3, wpg_ref[...], preferred_element_type=F32))
    pe = jnp.dot(p_ref[...].astype(BF16), wpp_ref[...], preferred_element_type=F32)
    x3 = x2 + gt * pe
    o_ref[...] = _rms(x3, gf_ref[...]) if final else x3


def _combine(pos, ys, x1, gates, p, gp, wpg, wpp, gf, *, tok_off, n_tok, final):
    D = x1.shape[1]
    tm = 256
    off = tok_off // tm
    return pl.pallas_call(
        functools.partial(_combine_kernel, tm=tm, final=final),
        grid=(n_tok // tm,),
        in_specs=[pl.BlockSpec((TOP_K * tm,), lambda i: (off + i,), memory_space=pltpu.SMEM),
                  pl.BlockSpec(memory_space=pl.ANY),
                  pl.BlockSpec((tm, D), lambda i: (off + i, 0)),
                  pl.BlockSpec((tm, LANE), lambda i: (off + i, 0)),
                  pl.BlockSpec((tm, p.shape[1]), lambda i: (off + i, 0)),
                  _resident((1, D)), _resident(wpg.shape), _resident(wpp.shape), _resident((1, D))],
        out_specs=pl.BlockSpec((tm, D), lambda i: (i, 0)),
        out_shape=jax.ShapeDtypeStruct((n_tok, D), F32),
        scratch_shapes=[pltpu.VMEM((tm * SUBLANE, LANE), U32)] * TOP_K
                       + [pltpu.VMEM((tm, D), F32), pltpu.SemaphoreType.DMA],
        compiler_params=_cparams("arbitrary"),
        name="moe_combine_ple",
    )(pos, ys, x1, gates, p, gp, wpg, wpp, gf)


def _route(ridx, n_exp):
    flat_e = ridx.reshape(-1)
    tk = flat_e.shape[0]
    oh = (flat_e[:, None] == jnp.arange(n_exp, dtype=I32)[None, :]).astype(I32)
    csum = jnp.cumsum(oh, axis=0)
    rank = jnp.sum(oh * csum, axis=1) - 1
    counts = csum[-1]
    pcounts = (counts + MOE_BM - 1) // MOE_BM * MOE_BM
    pend = jnp.cumsum(pcounts)
    pstart = pend - pcounts
    dest = (pstart[flat_e] + rank).astype(I32)
    nblk = (tk + n_exp * (MOE_BM - 1) + MOE_BM - 1) // MOE_BM
    blk_start = jnp.arange(nblk, dtype=I32) * MOE_BM
    block_e = jnp.minimum(jnp.sum((blk_start[:, None] >= pend[None, :]).astype(I32), axis=1), n_exp - 1)
    n_used = (pend[-1:] // MOE_BM).astype(I32)
    return dest, block_e, n_used, nblk * MOE_BM


def _swap_halves(w):
    half = w.shape[-1] // 2
    return jnp.concatenate([w[..., half:], w[..., :half]], axis=-1)


def _rope_tables(length):
    pos = jnp.arange(length, dtype=F32)
    inv_freq = ROPE_THETA ** (-jnp.arange(0, ROPE, 2, dtype=F32) / ROPE)
    ang = pos[:, None] * inv_freq[None, :]
    cos, sin = jnp.cos(ang), jnp.sin(ang)
    cos2 = jnp.concatenate([cos, cos], axis=-1)
    sin2 = jnp.concatenate([-sin, sin], axis=-1)
    zeros = jnp.zeros((length, LANE - ROPE), F32)
    return (jnp.concatenate([cos2, zeros], axis=-1), jnp.concatenate([sin2, zeros], axis=-1),
            cos2.T, sin2.T)


def kernel(x_prompt, x_sample, p_prompt, p_sample, g_mix, w_in, g_q_lat, w_q_up, g_kv_lat, w_kv_up, na_rpb, w_br_a, w_br_b, w_out, g_moe, w_router, b_router, w_gate, b_gate, w_up, b_up, w_down, b_down, g_ple, w_ple_gate, w_ple_proj, g_final):
    bp, lp, D = x_prompt.shape
    bs, ls, _ = x_sample.shape
    tp, ts = bp * lp, bs * ls
    groups = ((bp, lp), (bs, ls))
    depth = w_in.shape[0]
    na_heads = na_rpb.shape[1]
    na_w = na_heads * NA_DH
    ql = g_q_lat.shape[-1]
    kvl = g_kv_lat.shape[-1]
    mla_heads = w_q_up.shape[-1] // (NOPE + ROPE)
    n_exp = w_router.shape[-1]
    assert na_w == D and ql == kvl and lp >= ls

    xs_in = (x_prompt.reshape(tp, D), x_sample.reshape(ts, D))
    rope_tabs = _rope_tables(lp)
    outs = None
    for li in range(depth):
        p = jnp.concatenate([p_prompt[li].reshape(tp, -1), p_sample[li].reshape(ts, -1)], axis=0)
        wi = w_in[li]
        o = [0]
        for wdt in (na_w, na_w, na_w, ql, kvl, ROPE, D, D):
            o.append(o[-1] + wdt)
        w_qa, w_ka, w_va, w_cq, w_ckv, w_kr, w_ga, w_gb = (wi[:, o[k]:o[k + 1]] for k in range(8))
        w_main = jnp.concatenate([w_qa * (NA_DH ** -0.5 * LOG2E), w_ka, w_va, w_ga, w_gb, w_cq, w_ckv],
                                 axis=1).astype(BF16)
        ga_col = 3 * na_w
        cq_col = 3 * na_w + 2 * D
        w_kr2 = jnp.concatenate([w_kr, _swap_halves(w_kr)], axis=1).astype(BF16)
        wq3 = w_q_up[li].reshape(ql, mla_heads, NOPE + ROPE) * ((NOPE + ROPE) ** -0.5 * LOG2E)
        wqt = jnp.concatenate([wq3[..., :NOPE], wq3[..., NOPE:], _swap_halves(wq3[..., NOPE:])],
                              axis=-1).reshape(ql, mla_heads * HEAD_PAD).T.astype(BF16)
        wkv3 = w_kv_up[li].reshape(kvl, mla_heads, NOPE + V_DIM)
        wk = wkv3[..., :NOPE].reshape(kvl, mla_heads * NOPE).astype(BF16)
        wvt = wkv3[..., NOPE:].reshape(kvl, mla_heads * V_DIM).T.astype(BF16)
        bias = _na_bias(na_rpb[li])
        wr = jnp.pad(w_router[li], ((0, 0), (0, LANE - n_exp))).astype(BF16)
        br = jnp.pad(b_router[li].astype(F32), (0, LANE - n_exp), constant_values=NEG)[None, :]

        z, kr = _inproj(xs_in[0], xs_in[1], g_mix[li][None, :], w_main, w_kr2)
        qt, kp, vt = _mlaproj(z, kr, rope_tabs, g_q_lat[li][None, :], g_kv_lat[li][None, :], wqt, wk, wvt,
                              cq_col=cq_col, groups=groups, heads=mla_heads)
        ya = _na_attn(z, bias, groups=groups, heads=na_heads, k_col=na_w, v_col=2 * na_w)
        yb0 = _mla_attn(qt, kp, vt, batch=bp, length=lp, tok_off=0, heads=mla_heads)
        yb1 = _mla_attn(qt, kp, vt, batch=bs, length=ls, tok_off=tp, heads=mla_heads)
        x1, h2, ridx, rgate = _merge(ya, yb0, yb1, z, xs_in[0], xs_in[1], w_br_a[li].astype(BF16),
                                     w_br_b[li].astype(BF16), w_out[li].astype(BF16), g_moe[li][None, :],
                                     wr, br, ga_col=ga_col)

        dest, block_e, n_used, n_rows = _route(ridx[:, :TOP_K], n_exp)
        xs = _dispatch(dest, h2, n_rows)
        ys = _experts(block_e, n_used, xs,
                      w_gate[li].astype(BF16), b_gate[li][:, None, :], w_up[li].astype(BF16),
                      b_up[li][:, None, :], w_down[li].astype(BF16), b_down[li][:, None, :])
        final = li == depth - 1
        comb = functools.partial(_combine, dest, ys, x1, rgate, p, g_ple[li][None, :],
                                 w_ple_gate[li].astype(BF16), w_ple_proj[li].astype(BF16),
                                 g_final[None, :], final=final)
        outs = (comb(tok_off=0, n_tok=tp), comb(tok_off=tp, n_tok=ts))
        xs_in = outs
    return (outs[0].reshape(bp, lp, D), outs[1].reshape(bs, ls, D))
```

```python
import functools

import jax
import jax.numpy as jnp
from jax import lax
from jax.experimental import pallas as pl
from jax.experimental.pallas import tpu as pltpu

F32 = jnp.float32
BF16 = jnp.bfloat16
I32 = jnp.int32
U32 = jnp.uint32

LANE = 128
SUBLANE = 8
ROW_WORDS = SUBLANE * LANE
VMEM_LIMIT_BYTES = 56 * 1024 * 1024

RMS_EPS = 1e-6
GRID_W = 64
NA_KH = 8
NA_KW = 16
NA_DH = 128
NA_QROWS = 8
NA_QTOK = NA_QROWS * GRID_W
NA_WTOK = 2 * NA_QTOK
NA_PIECE = NA_WTOK // 4
NA_SM_STRIP = 128
NA_HEADS_PER_STEP = 8
NOPE = 128
ROPE = 64
V_DIM = 128
HEAD_PAD = 256
V_PAD = 144
ROPE_THETA = 10000.0
TOP_K = 4
SWIGLU_LIMIT = 7.0
SWIGLU_ALPHA = 1.702
MOE_BM = 512
MLA_TQ_CANDS = (2048, 1024, 512)
MLA_TK_CANDS = (512, 256)
MLA_SM_STRIP = 256
LOG2E = 1.4426950408889634
NEG = -1e30


def _cparams(*sem):
    return pltpu.CompilerParams(dimension_semantics=sem, vmem_limit_bytes=VMEM_LIMIT_BYTES)


def _pick(n, cands):
    for c in cands:
        if n % c == 0:
            return c
    raise ValueError(f"no tile in {cands} divides {n}")


def _resident(shape):
    nd = len(shape)
    return pl.BlockSpec(shape, lambda *_: (0,) * nd, pipeline_mode=pl.Buffered(1))


def _rms(x, g):
    return x * lax.rsqrt(jnp.mean(x * x, axis=-1, keepdims=True) + RMS_EPS) * g


def _sigmoid(x):
    return 1.0 / (1.0 + jnp.exp(-x))


def _pack_pair(lo, hi):
    lo_bits = lax.bitcast_convert_type(lo.astype(BF16).astype(F32), U32) >> 16
    hi_bits = lax.bitcast_convert_type(hi.astype(BF16).astype(F32), U32)
    return lo_bits | hi_bits


def _unpack_pair(w):
    lo = lax.bitcast_convert_type(w << 16, F32)
    hi = lax.bitcast_convert_type(w & jnp.uint32(0xFFFF0000), F32)
    return lo, hi


def _store_packed_rows(ref, x):
    rows, d = x.shape
    assert d == 2 * ROW_WORDS
    for i in range(SUBLANE):
        lo = x[:, i * LANE:(i + 1) * LANE]
        hi = x[:, d // 2 + i * LANE:d // 2 + (i + 1) * LANE]
        ref[pl.ds(i, rows, stride=SUBLANE), :] = _pack_pair(lo, hi)


def _load_packed_rows(ref, rows):
    half = ROW_WORDS
    for i in range(SUBLANE):
        lo, hi = _unpack_pair(ref[pl.ds(i, rows, stride=SUBLANE), :])
        yield i * LANE, lo
        yield half + i * LANE, hi


def _group_specs(block, n_first, **kw):
    first = pl.BlockSpec(block, lambda i, *_: (jnp.minimum(i, n_first - 1), 0), **kw)
    second = pl.BlockSpec(block, lambda i, *_: (jnp.maximum(i - n_first, 0), 0), **kw)
    return [first, second]


def _inproj_kernel(xa_ref, xb_ref, g_ref, w_ref, wkr_ref, z_ref, kr_ref, h_scr, *, n_first):
    @pl.when(pl.program_id(1) == 0)
    def _():
        x = jnp.where(pl.program_id(0) < n_first, xa_ref[...], xb_ref[...])
        hb = _rms(x, g_ref[...]).astype(BF16)
        h_scr[...] = hb
        kr_ref[...] = jnp.dot(hb, wkr_ref[...], preferred_element_type=F32)

    z_ref[...] = jnp.dot(h_scr[...], w_ref[...], preferred_element_type=F32).astype(z_ref.dtype)


def _inproj(xa, xb, g, w, wkr):
    D = xa.shape[1]
    T = xa.shape[0] + xb.shape[0]
    N = w.shape[1]
    tm = _pick(xb.shape[0], (1024, 512, 256))
    assert xa.shape[0] % tm == 0
    tn = _pick(N, (1024, 512, 256, 128))
    n_first = xa.shape[0] // tm
    return pl.pallas_call(
        functools.partial(_inproj_kernel, n_first=n_first),
        grid=(T // tm, N // tn),
        in_specs=_group_specs((tm, D), n_first, pipeline_mode=pl.Buffered(1)) + [
                  pl.BlockSpec((1, D), lambda i, j: (0, 0)),
                  pl.BlockSpec((D, tn), lambda i, j: (0, j)),
                  pl.BlockSpec((D, LANE), lambda i, j: (0, 0))],
        out_specs=[pl.BlockSpec((tm, tn), lambda i, j: (i, j)),
                   pl.BlockSpec((tm, LANE), lambda i, j: (i, 0))],
        out_shape=[jax.ShapeDtypeStruct((T, N), BF16), jax.ShapeDtypeStruct((T, LANE), F32)],
        scratch_shapes=[pltpu.VMEM((tm, D), BF16)],
        compiler_params=_cparams("parallel", "arbitrary"),
        name="inproj",
    )(xa, xb, g, w, wkr)


def _mlaproj_kernel(cq_ref, ckv_ref, kr_ref, cos_ref, sin_ref, cost_ref, sint_ref, gq_ref, gkv_ref,
                    wqt_ref, wk_ref, wvt_ref, qt_out, k_out, vt_out, *, heads):
    tm = cq_ref.shape[0]
    cqn = _rms(cq_ref[...].astype(F32), gq_ref[...]).astype(BF16)
    ckvn = _rms(ckv_ref[...].astype(F32), gkv_ref[...]).astype(BF16)
    ckvn_t = ckvn.T
    qt = jnp.dot(wqt_ref[...], cqn.T, preferred_element_type=F32)
    kn = jnp.dot(ckvn, wk_ref[...], preferred_element_type=F32)
    vt = jnp.dot(wvt_ref[...], ckvn_t, preferred_element_type=F32)
    kr = kr_ref[...]
    krope = (kr * cos_ref[...] + pltpu.roll(kr, ROPE, axis=1) * sin_ref[...]).astype(BF16)
    cos_t = cost_ref[...]
    sin_t = sint_ref[...]
    tail = (lax.broadcasted_iota(I32, (V_PAD - V_DIM, tm), 0) == 0).astype(BF16)
    for h in range(heads):
        c = h * HEAD_PAD
        qt_out[c:c + NOPE, :] = qt[c:c + NOPE].astype(BF16)
        x = qt[c + NOPE:c + NOPE + ROPE]
        x_swapped = qt[c + NOPE + ROPE:c + HEAD_PAD]
        qt_out[c + NOPE:c + NOPE + ROPE, :] = (x * cos_t + x_swapped * sin_t).astype(BF16)
        qt_out[c + NOPE + ROPE:c + HEAD_PAD, :] = jnp.zeros((ROPE, tm), BF16)
        k_out[:, c:c + NOPE] = kn[:, h * NOPE:(h + 1) * NOPE].astype(BF16)
        k_out[:, c + NOPE:c + HEAD_PAD] = krope
        vt_out[h * V_PAD:h * V_PAD + V_DIM, :] = vt[h * V_DIM:(h + 1) * V_DIM].astype(BF16)
        vt_out[h * V_PAD + V_DIM:(h + 1) * V_PAD, :] = tail


def _mlaproj(z, kr, tabs, gq, gkv, wqt, wk, wvt, *, cq_col, groups, heads):
    T = z.shape[0]
    ql = gq.shape[-1]
    kvl = gkv.shape[-1]
    W = heads * HEAD_PAD
    (bp, lp), (bs, ls) = groups
    tm = _pick(ls, (256, 128))
    n_p = bp * lp // tm
    cos_n, sin_n, cos_t, sin_t = tabs

    def pos_blk(i):
        return jnp.where(i < n_p, i % (lp // tm), (i - n_p) % (ls // tm))

    return pl.pallas_call(
        functools.partial(_mlaproj_kernel, heads=heads),
        grid=(T // tm,),
        in_specs=[pl.BlockSpec((tm, ql), lambda i: (i, cq_col // ql)),
                  pl.BlockSpec((tm, kvl), lambda i: (i, (cq_col + ql) // kvl)),
                  pl.BlockSpec((tm, LANE), lambda i: (i, 0)),
                  pl.BlockSpec((tm, LANE), lambda i: (pos_blk(i), 0)),
                  pl.BlockSpec((tm, LANE), lambda i: (pos_blk(i), 0)),
                  pl.BlockSpec((ROPE, tm), lambda i: (0, pos_blk(i))),
                  pl.BlockSpec((ROPE, tm), lambda i: (0, pos_blk(i))),
                  _resident((1, ql)), _resident((1, kvl)),
                  _resident(wqt.shape), _resident(wk.shape), _resident(wvt.shape)],
        out_specs=[pl.BlockSpec((W, tm), lambda i: (0, i)),
                   pl.BlockSpec((tm, W), lambda i: (i, 0)),
                   pl.BlockSpec((heads * V_PAD, tm), lambda i: (0, i))],
        out_shape=[jax.ShapeDtypeStruct((W, T), BF16), jax.ShapeDtypeStruct((T, W), BF16),
                   jax.ShapeDtypeStruct((heads * V_PAD, T), BF16)],
        compiler_params=_cparams("parallel"),
        name="mlaproj",
    )(z, z, kr, cos_n, sin_n, cos_t, sin_t, gq, gkv, wqt, wk, wvt)


def _mla_attn_kernel(q_ref, k_ref, v_ref, o_ref, s0, s1, p0, p1, a0, a1, m_scr, acc_scr, *, tk, nk):
    qt = q_ref[...]

    def qk(c, s_ref):
        off = pl.multiple_of(c * tk, tk)
        s_ref[...] = jnp.dot(k_ref[pl.ds(off, tk), :], qt, preferred_element_type=F32)

    def sm(s_ref, p_ref, a_ref):
        for j in range(s_ref.shape[1] // MLA_SM_STRIP):
            sl = slice(j * MLA_SM_STRIP, (j + 1) * MLA_SM_STRIP)
            m_prev = m_scr[:, sl]
            m_new = jnp.maximum(m_prev, jnp.max(s_ref[:, sl], axis=0, keepdims=True))
            a_ref[:, sl] = jnp.exp2(m_prev - m_new)
            p_ref[:, sl] = jnp.exp2(s_ref[:, sl] - m_new).astype(BF16)
            m_scr[:, sl] = m_new

    def pv(c, p_ref, a_ref):
        off = pl.multiple_of(c * tk, tk)
        acc_scr[...] = acc_scr[...] * a_ref[...] + jnp.dot(v_ref[:, pl.ds(off, tk)], p_ref[...],
                                                           preferred_element_type=F32)

    m_scr[...] = jnp.full(m_scr.shape, NEG, F32)
    acc_scr[...] = jnp.zeros(acc_scr.shape, F32)
    qk(0, s0)
    qk(1, s1)
    sm(s0, p0, a0)

    def body(j, carry):
        qk(2 * j, s0)
        sm(s1, p1, a1)
        pv(2 * j - 2, p0, a0)
        qk(2 * j + 1, s1)
        sm(s0, p0, a0)
        pv(2 * j - 1, p1, a1)
        return carry

    lax.fori_loop(1, nk // 2, body, 0)
    sm(s1, p1, a1)
    pv(nk - 2, p0, a0)
    pv(nk - 1, p1, a1)
    acc = acc_scr[...]
    o_ref[...] = (acc[:V_DIM] / acc[V_DIM:V_DIM + 1]).T.astype(o_ref.dtype)


def _mla_attn(qt, kp, vt, *, batch, length, tok_off, heads):
    tq = _pick(length, MLA_TQ_CANDS)
    tk = _pick(length // 2, MLA_TK_CANDS)
    nk = length // tk
    assert nk % 2 == 0 and tok_off % length == 0
    nq = length // tq
    qoff = tok_off // tq
    boff = tok_off // length
    return pl.pallas_call(
        functools.partial(_mla_attn_kernel, tk=tk, nk=nk),
        grid=(batch, heads, nq),
        in_specs=[pl.BlockSpec((HEAD_PAD, tq), lambda b, h, i: (h, qoff + b * nq + i)),
                  pl.BlockSpec((length, HEAD_PAD), lambda b, h, i: (boff + b, h)),
                  pl.BlockSpec((V_PAD, length), lambda b, h, i: (h, boff + b))],
        out_specs=pl.BlockSpec((tq, V_DIM), lambda b, h, i: (b * nq + i, h)),
        out_shape=jax.ShapeDtypeStruct((batch * length, heads * V_DIM), BF16),
        scratch_shapes=[pltpu.VMEM((tk, tq), F32), pltpu.VMEM((tk, tq), F32),
                        pltpu.VMEM((tk, tq), BF16), pltpu.VMEM((tk, tq), BF16),
                        pltpu.VMEM((1, tq), F32), pltpu.VMEM((1, tq), F32),
                        pltpu.VMEM((1, tq), F32), pltpu.VMEM((V_PAD, tq), F32)],
        compiler_params=_cparams("parallel", "parallel", "arbitrary"),
        name="mla_attn",
    )(qt, kp, vt)


def _na_kernel(q_ref, k0, k1, k2, k3, v0, v1, v2, v3, b_ref, o_ref, s_scr, p_scr, den_scr):
    pw = NA_PIECE
    for hh in range(NA_HEADS_PER_STEP):
        cols = slice(hh * NA_DH, (hh + 1) * NA_DH)
        qt = q_ref[:, cols].T
        for j, kr in enumerate((k0, k1, k2, k3)):
            s_scr[j * pw:(j + 1) * pw, :] = (jnp.dot(kr[:, cols], qt, preferred_element_type=F32)
                                             + b_ref[hh, j * pw:(j + 1) * pw, :])
        for t in range(q_ref.shape[0] // NA_SM_STRIP):
            sl = slice(t * NA_SM_STRIP, (t + 1) * NA_SM_STRIP)
            m = jnp.max(s_scr[:, sl], axis=0, keepdims=True)
            p = jnp.exp2(s_scr[:, sl] - m)
            den_scr[:, sl] = jnp.sum(p, axis=0, keepdims=True)
            p_scr[:, sl] = p.astype(BF16)
        acc = None
        for j, vr in enumerate((v0, v1, v2, v3)):
            a = jnp.dot(vr[:, cols].T, p_scr[j * pw:(j + 1) * pw, :], preferred_element_type=F32)
            acc = a if acc is None else acc + a
        o_ref[:, cols] = (acc / den_scr[...]).T.astype(o_ref.dtype)


def _na_bias(rpb):
    H = rpb.shape[0]
    a = jnp.arange(NA_QROWS)
    j = jnp.arange(2 * NA_QROWS)
    c = jnp.arange(GRID_W)
    q_rel = jnp.stack([a, a + NA_KH // 2, a + NA_QROWS])
    rs_rel = jnp.stack([jnp.maximum(a - NA_KH // 2, 0), a, jnp.minimum(a + NA_KH // 2, NA_QROWS)])
    vrow = (j[None, None, :] >= rs_rel[:, :, None]) & (j[None, None, :] < rs_rel[:, :, None] + NA_KH)
    dy = jnp.clip(j[None, None, :] - q_rel[:, :, None] + NA_KH - 1, 0, 2 * NA_KH - 2)
    cs = jnp.clip(c - NA_KW // 2, 0, GRID_W - NA_KW)
    vcol = (c[None, :] >= cs[:, None]) & (c[None, :] < cs[:, None] + NA_KW)
    dx = jnp.clip(c[None, :] - c[:, None] + NA_KW - 1, 0, 2 * NA_KW - 2)
    oh_x = (dx[:, :, None] == jnp.arange(2 * NA_KW - 1)[None, None, :]).astype(F32)
    oh_y = (dy[..., None] == jnp.arange(2 * NA_KH - 1)[None, None, None, :]).astype(F32)
    b2 = jnp.einsum("hyx,cdx->hycd", rpb.astype(F32), oh_x, precision=lax.Precision.HIGHEST)
    b = jnp.einsum("kajy,hycd->hkacjd", oh_y, b2, precision=lax.Precision.HIGHEST)
    valid = vrow[:, :, None, :, None] & vcol[None, None, :, None, :]
    b = jnp.where(valid[None], b * LOG2E, NEG)
    return b.transpose(1, 0, 4, 5, 2, 3).reshape(3, H, NA_WTOK, NA_QTOK)


def _na_attn(z, bias, *, groups, heads, k_col, v_col):
    T = z.shape[0]
    (bp, lp), (bs, ls) = groups
    n_p = bp * lp // NA_QTOK
    nbp = lp // NA_QTOK
    nbs = ls // NA_QTOK
    assert nbp >= 2 and nbs >= 2, "needs at least 16 grid rows per sequence"
    piece_per_blk = NA_QTOK // NA_PIECE

    def meta(g):
        is_p = g < n_p
        gl = jnp.where(is_p, g, g - n_p)
        nb = jnp.where(is_p, nbp, nbs)
        seq = gl // nb
        i = gl % nb
        base = jnp.where(is_p, 0, bp * lp // NA_PIECE) + seq * (piece_per_blk * nb)
        w0 = jnp.clip(piece_per_blk * i - 1, 0, piece_per_blk * nb - 4)
        kind = jnp.where(i == 0, 0, jnp.where(i == nb - 1, 2, 1))
        return base + w0, kind

    hb = NA_HEADS_PER_STEP
    wcols = hb * NA_DH
    assert heads % hb == 0 and k_col % wcols == 0 and v_col % wcols == 0

    def piece_spec(col, jj):
        return pl.BlockSpec((NA_PIECE, wcols), lambda h, g: (meta(g)[0] + jj, col + h))

    return pl.pallas_call(
        _na_kernel,
        grid=(heads // hb, T // NA_QTOK),
        in_specs=([pl.BlockSpec((NA_QTOK, wcols), lambda h, g: (g, h))]
                  + [piece_spec(k_col // wcols, jj) for jj in range(4)]
                  + [piece_spec(v_col // wcols, jj) for jj in range(4)]
                  + [pl.BlockSpec((None, hb, NA_WTOK, NA_QTOK), lambda h, g: (meta(g)[1], h, 0, 0))]),
        out_specs=pl.BlockSpec((NA_QTOK, wcols), lambda h, g: (g, h)),
        out_shape=jax.ShapeDtypeStruct((T, heads * NA_DH), BF16),
        scratch_shapes=[pltpu.VMEM((NA_WTOK, NA_QTOK), F32), pltpu.VMEM((NA_WTOK, NA_QTOK), BF16),
                        pltpu.VMEM((1, NA_QTOK), F32)],
        compiler_params=_cparams("parallel", "parallel"),
        name="na_attn",
    )(z, *([z] * 8), bias)


def _merge_kernel(ya_ref, yb0_ref, yb1_ref, ga_ref, gb_ref, x0_ref, x1in_ref, wa_ref, wb_ref, wo_ref, gm_ref,
                  wr_ref, br_ref, x1_ref, h2_ref, ridx_ref, rgate_ref, *, n_first):
    in_first = pl.program_id(0) < n_first
    ya = jnp.dot(ya_ref[...], wa_ref[...], preferred_element_type=F32)
    yb = jnp.dot(jnp.where(in_first, yb0_ref[...], yb1_ref[...]), wb_ref[...], preferred_element_type=F32)
    u = _sigmoid(ga_ref[...].astype(F32)) * ya + _sigmoid(gb_ref[...].astype(F32)) * yb
    x = jnp.where(in_first, x0_ref[...], x1in_ref[...])
    x1 = x + jnp.dot(u.astype(BF16), wo_ref[...], preferred_element_type=F32)
    x1_ref[...] = x1
    h2 = _rms(x1, gm_ref[...])
    _store_packed_rows(h2_ref, h2)
    logits = jnp.dot(h2.astype(BF16), wr_ref[...], preferred_element_type=F32) + br_ref[...]
    lane = lax.broadcasted_iota(I32, logits.shape, 1)
    vals, idxs = [], []
    for _ in range(TOP_K):
        mx = jnp.max(logits, axis=-1, keepdims=True)
        ix = jnp.min(jnp.where(logits == mx, lane, LANE), axis=-1, keepdims=True)
        vals.append(mx)
        idxs.append(ix)
        logits = jnp.where(lane == ix, 2 * NEG, logits)
    es = [jnp.exp(v - vals[0]) for v in vals]
    den = functools.reduce(lambda a, b: a + b, es)
    ridx = jnp.zeros(lane.shape, I32)
    rgate = jnp.zeros(lane.shape, F32)
    for k in range(TOP_K):
        ridx = jnp.where(lane == k, idxs[k], ridx)
        rgate = jnp.where(lane == k, es[k] / den, rgate)
    ridx_ref[...] = ridx
    rgate_ref[...] = rgate


def _merge(ya, yb0, yb1, z, x0, x1, wa, wb, wo, gm, wr, br, *, ga_col):
    D = x0.shape[1]
    T = x0.shape[0] + x1.shape[0]
    tm = 256
    n_first = x0.shape[0] // tm
    return pl.pallas_call(
        functools.partial(_merge_kernel, n_first=n_first),
        grid=(T // tm,),
        in_specs=[pl.BlockSpec((tm, D), lambda i: (i, 0))]
                 + _group_specs((tm, D), n_first)
                 + [pl.BlockSpec((tm, D), lambda i: (i, ga_col // D)),
                    pl.BlockSpec((tm, D), lambda i: (i, ga_col // D + 1))]
                 + _group_specs((tm, D), n_first)
                 + [_resident(wa.shape), _resident(wb.shape), _resident(wo.shape),
                    _resident((1, D)), _resident((D, LANE)), _resident((1, LANE))],
        out_specs=[pl.BlockSpec((tm, D), lambda i: (i, 0)),
                   pl.BlockSpec((tm * SUBLANE, LANE), lambda i: (i, 0)),
                   pl.BlockSpec((tm, LANE), lambda i: (i, 0)),
                   pl.BlockSpec((tm, LANE), lambda i: (i, 0))],
        out_shape=[jax.ShapeDtypeStruct((T, D), F32), jax.ShapeDtypeStruct((T * SUBLANE, LANE), U32),
                   jax.ShapeDtypeStruct((T, LANE), I32), jax.ShapeDtypeStruct((T, LANE), F32)],
        compiler_params=_cparams("parallel"),
        name="merge_router",
    )(ya, yb0, yb1, z, z, x0, x1, wa, wb, wo, gm, wr, br)


def _row_tile(ref, r):
    return ref.at[pl.ds(pl.multiple_of(r * SUBLANE, SUBLANE), SUBLANE), :]


def _dispatch_kernel(dest_ref, h_ref, xs_in_ref, xs_ref, sem, *, tm):
    del xs_in_ref

    def body(r, c):
        for k in range(TOP_K):
            pltpu.make_async_copy(_row_tile(h_ref, r), _row_tile(xs_ref, dest_ref[TOP_K * r + k]), sem).start()
        return c

    lax.fori_loop(0, tm, body, 0)
    for _ in range(TOP_K):
        pltpu.make_async_copy(h_ref, xs_ref.at[pl.ds(0, tm * SUBLANE), :], sem).wait()


def _dispatch(dest, h2p, n_rows):
    T = h2p.shape[0] // SUBLANE
    tm = 256
    xs0 = jnp.zeros((n_rows * SUBLANE, LANE), U32)
    return pl.pallas_call(
        functools.partial(_dispatch_kernel, tm=tm),
        grid=(T // tm,),
        in_specs=[pl.BlockSpec((TOP_K * tm,), lambda i: (i,), memory_space=pltpu.SMEM),
                  pl.BlockSpec((tm * SUBLANE, LANE), lambda i: (i, 0)),
                  pl.BlockSpec(memory_space=pl.ANY)],
        out_specs=pl.BlockSpec(memory_space=pl.ANY),
        out_shape=jax.ShapeDtypeStruct((n_rows * SUBLANE, LANE), U32),
        scratch_shapes=[pltpu.SemaphoreType.DMA],
        input_output_aliases={2: 0},
        compiler_params=_cparams("arbitrary"),
        name="moe_dispatch",
    )(dest, h2p, xs0)


def _expert_kernel(be_ref, nu_ref, xs_ref, wg_ref, bg_ref, wu_ref, bu_ref, wd_ref, bd_ref, ys_ref,
                   xb_scr, acc_scr, *, nf):
    i = pl.program_id(0)
    f = pl.program_id(1)
    bm = xb_scr.shape[0]
    used = i < nu_ref[0]

    @pl.when(used)
    def _():
        @pl.when(f == 0)
        def _():
            for col, blk in _load_packed_rows(xs_ref, bm):
                xb_scr[:, col:col + LANE] = blk.astype(BF16)
            acc_scr[...] = jnp.broadcast_to(bd_ref[0], acc_scr.shape)

        xb = xb_scr[...]
        g = jnp.dot(xb, wg_ref[0, 0], preferred_element_type=F32) + bg_ref[0]
        u = jnp.dot(xb, wu_ref[0, 0], preferred_element_type=F32) + bu_ref[0]
        g = jnp.minimum(g, SWIGLU_LIMIT)
        u = jnp.clip(u, -SWIGLU_LIMIT, SWIGLU_LIMIT)
        a = g * _sigmoid(SWIGLU_ALPHA * g) * (u + 1.0)
        acc_scr[...] += jnp.dot(a.astype(BF16), wd_ref[0], preferred_element_type=F32)

    @pl.when(f == nf - 1)
    def _():
        @pl.when(used)
        def _():
            _store_packed_rows(ys_ref, acc_scr[...])

        @pl.when(jnp.logical_not(used))
        def _():
            ys_ref[...] = jnp.zeros(ys_ref.shape, U32)


def _expert_tf(f_dim):
    return _pick(f_dim, (1024, 512, 256, 128))


def _tile_columns(w, tf):
    e, d, f = w.shape
    return w.reshape(e, d, f // tf, tf).transpose(0, 2, 1, 3)


def _experts(block_e, n_used, xs, wg, bg, wu, bu, wd, bd):
    P = xs.shape[0] // SUBLANE
    E, nf, D, tf = wg.shape
    nblk = P // MOE_BM

    def blk(i, nu):
        return jnp.minimum(i, nu[0] - 1)

    def fidx(i, f, nu):
        return jnp.where(i < nu[0], f, nf - 1)

    grid_spec = pltpu.PrefetchScalarGridSpec(
        num_scalar_prefetch=2,
        grid=(nblk, nf),
        in_specs=[pl.BlockSpec((MOE_BM * SUBLANE, LANE), lambda i, f, be, nu: (blk(i, nu), 0)),
                  pl.BlockSpec((1, 1, D, tf), lambda i, f, be, nu: (be[blk(i, nu)], fidx(i, f, nu), 0, 0)),
                  pl.BlockSpec((1, 1, tf), lambda i, f, be, nu: (be[blk(i, nu)], 0, fidx(i, f, nu))),
                  pl.BlockSpec((1, 1, D, tf), lambda i, f, be, nu: (be[blk(i, nu)], fidx(i, f, nu), 0, 0)),
                  pl.BlockSpec((1, 1, tf), lambda i, f, be, nu: (be[blk(i, nu)], 0, fidx(i, f, nu))),
                  pl.BlockSpec((1, tf, D), lambda i, f, be, nu: (be[blk(i, nu)], fidx(i, f, nu), 0)),
                  pl.BlockSpec((1, 1, D), lambda i, f, be, nu: (be[blk(i, nu)], 0, 0))],
        out_specs=pl.BlockSpec((MOE_BM * SUBLANE, LANE), lambda i, f, be, nu: (i, 0)),
        scratch_shapes=[pltpu.VMEM((MOE_BM, D), BF16), pltpu.VMEM((MOE_BM, D), F32)],
    )
    return pl.pallas_call(
        functools.partial(_expert_kernel, nf=nf),
        grid_spec=grid_spec,
        out_shape=jax.ShapeDtypeStruct((P * SUBLANE, LANE), U32),
        compiler_params=_cparams("arbitrary", "arbitrary"),
        name="moe_experts",
    )(block_e, n_used, xs, wg, bg, wu, bu, wd, bd)


def _combine_kernel(pos_ref, ys_ref, x1_ref, gate_ref, p_ref, gp_ref, wpg_ref, wpp_ref, gf_ref, o_ref,
                    b0, b1, b2, b3, x2_scr, sem, *, tm, final):
    bufs = (b0, b1, b2, b3)

    def body(r, c):
        for k in range(TOP_K):
            pltpu.make_async_copy(_row_tile(ys_ref, pos_ref[TOP_K * r + k]), _row_tile(bufs[k], r), sem).start()
        return c

    lax.fori_loop(0, tm, body, 0)
    for k in range(TOP_K):
        pltpu.make_async_copy(ys_ref.at[pl.ds(0, tm * SUBLANE), :], bufs[k], sem).wait()

    gates = gate_ref[...]
    x2_scr[...] = x1_ref[...]
    for k in range(TOP_K):
        g = gates[:, k:k + 1]
        for col, blk in _load_packed_rows(bufs[k], tm):
            x2_scr[:, col:col + LANE] += g * blk
    x2 = x2_scr[...]
    h3 = _rms(x2, gp_ref[...]).astype(BF16)
    gt = _sigmoid(jnp.dot(h3, wpg_ref[...], preferred_element_type=F32))
    pe = jnp.dot(p_ref[...].astype(BF16), wpp_ref[...], preferred_element_type=F32)
    x3 = x2 + gt * pe
    o_ref[...] = _rms(x3, gf_ref[...]) if final else x3


def _combine(pos, ys, x1, gates, p, gp, wpg, wpp, gf, *, tok_off, n_tok, final):
    D = x1.shape[1]
    tm = 256
    off = tok_off // tm
    return pl.pallas_call(
        functools.partial(_combine_kernel, tm=tm, final=final),
        grid=(n_tok // tm,),
        in_specs=[pl.BlockSpec((TOP_K * tm,), lambda i: (off + i,), memory_space=pltpu.SMEM),
                  pl.BlockSpec(memory_space=pl.ANY),
                  pl.BlockSpec((tm, D), lambda i: (off + i, 0)),
                  pl.BlockSpec((tm, LANE), lambda i: (off + i, 0)),
                  pl.BlockSpec((tm, p.shape[1]), lambda i: (off + i, 0)),
                  _resident((1, D)), _resident(wpg.shape), _resident(wpp.shape), _resident((1, D))],
        out_specs=pl.BlockSpec((tm, D), lambda i: (i, 0)),
        out_shape=jax.ShapeDtypeStruct((n_tok, D), F32),
        scratch_shapes=[pltpu.VMEM((tm * SUBLANE, LANE), U32)] * TOP_K
                       + [pltpu.VMEM((tm, D), F32), pltpu.SemaphoreType.DMA],
        compiler_params=_cparams("arbitrary"),
        name="moe_combine_ple",
    )(pos, ys, x1, gates, p, gp, wpg, wpp, gf)


def _route(ridx, n_exp):
    flat_e = ridx.reshape(-1)
    tk = flat_e.shape[0]
    oh = (flat_e[:, None] == jnp.arange(n_exp, dtype=I32)[None, :]).astype(I32)
    csum = jnp.cumsum(oh, axis=0)
    rank = jnp.sum(oh * csum, axis=1) - 1
    counts = csum[-1]
    pcounts = (counts + MOE_BM - 1) // MOE_BM * MOE_BM
    pend = jnp.cumsum(pcounts)
    pstart = pend - pcounts
    dest = (pstart[flat_e] + rank).astype(I32)
    nblk = (tk + n_exp * (MOE_BM - 1) + MOE_BM - 1) // MOE_BM
    blk_start = jnp.arange(nblk, dtype=I32) * MOE_BM
    block_e = jnp.minimum(jnp.sum((blk_start[:, None] >= pend[None, :]).astype(I32), axis=1), n_exp - 1)
    n_used = (pend[-1:] // MOE_BM).astype(I32)
    return dest, block_e, n_used, nblk * MOE_BM


def _swap_halves(w):
    half = w.shape[-1] // 2
    return jnp.concatenate([w[..., half:], w[..., :half]], axis=-1)


def _rope_tables(length):
    pos = jnp.arange(length, dtype=F32)
    inv_freq = ROPE_THETA ** (-jnp.arange(0, ROPE, 2, dtype=F32) / ROPE)
    ang = pos[:, None] * inv_freq[None, :]
    cos, sin = jnp.cos(ang), jnp.sin(ang)
    cos2 = jnp.concatenate([cos, cos], axis=-1)
    sin2 = jnp.concatenate([-sin, sin], axis=-1)
    zeros = jnp.zeros((length, LANE - ROPE), F32)
    return (jnp.concatenate([cos2, zeros], axis=-1), jnp.concatenate([sin2, zeros], axis=-1),
            cos2.T, sin2.T)


def kernel(x_prompt, x_sample, p_prompt, p_sample, g_mix, w_in, g_q_lat, w_q_up, g_kv_lat, w_kv_up, na_rpb, w_br_a, w_br_b, w_out, g_moe, w_router, b_router, w_gate, b_gate, w_up, b_up, w_down, b_down, g_ple, w_ple_gate, w_ple_proj, g_final):
    bp, lp, D = x_prompt.shape
    bs, ls, _ = x_sample.shape
    tp, ts = bp * lp, bs * ls
    groups = ((bp, lp), (bs, ls))
    depth = w_in.shape[0]
    na_heads = na_rpb.shape[1]
    na_w = na_heads * NA_DH
    ql = g_q_lat.shape[-1]
    kvl = g_kv_lat.shape[-1]
    mla_heads = w_q_up.shape[-1] // (NOPE + ROPE)
    n_exp = w_router.shape[-1]
    assert na_w == D and ql == kvl and lp >= ls

    xs_in = (x_prompt.reshape(tp, D), x_sample.reshape(ts, D))
    rope_tabs = _rope_tables(lp)
    outs = None
    for li in range(depth):
        p = jnp.concatenate([p_prompt[li].reshape(tp, -1), p_sample[li].reshape(ts, -1)], axis=0)
        wi = w_in[li]
        o = [0]
        for wdt in (na_w, na_w, na_w, ql, kvl, ROPE, D, D):
            o.append(o[-1] + wdt)
        w_qa, w_ka, w_va, w_cq, w_ckv, w_kr, w_ga, w_gb = (wi[:, o[k]:o[k + 1]] for k in range(8))
        w_main = jnp.concatenate([w_qa * (NA_DH ** -0.5 * LOG2E), w_ka, w_va, w_ga, w_gb, w_cq, w_ckv],
                                 axis=1).astype(BF16)
        ga_col = 3 * na_w
        cq_col = 3 * na_w + 2 * D
        w_kr2 = jnp.concatenate([w_kr, _swap_halves(w_kr)], axis=1).astype(BF16)
        wq3 = w_q_up[li].reshape(ql, mla_heads, NOPE + ROPE) * ((NOPE + ROPE) ** -0.5 * LOG2E)
        wqt = jnp.concatenate([wq3[..., :NOPE], wq3[..., NOPE:], _swap_halves(wq3[..., NOPE:])],
                              axis=-1).reshape(ql, mla_heads * HEAD_PAD).T.astype(BF16)
        wkv3 = w_kv_up[li].reshape(kvl, mla_heads, NOPE + V_DIM)
        wk = wkv3[..., :NOPE].reshape(kvl, mla_heads * NOPE).astype(BF16)
        wvt = wkv3[..., NOPE:].reshape(kvl, mla_heads * V_DIM).T.astype(BF16)
        bias = _na_bias(na_rpb[li])
        wr = jnp.pad(w_router[li], ((0, 0), (0, LANE - n_exp))).astype(BF16)
        br = jnp.pad(b_router[li].astype(F32), (0, LANE - n_exp), constant_values=NEG)[None, :]

        z, kr = _inproj(xs_in[0], xs_in[1], g_mix[li][None, :], w_main, w_kr2)
        qt, kp, vt = _mlaproj(z, kr, rope_tabs, g_q_lat[li][None, :], g_kv_lat[li][None, :], wqt, wk, wvt,
                              cq_col=cq_col, groups=groups, heads=mla_heads)
        ya = _na_attn(z, bias, groups=groups, heads=na_heads, k_col=na_w, v_col=2 * na_w)
        yb0 = _mla_attn(qt, kp, vt, batch=bp, length=lp, tok_off=0, heads=mla_heads)
        yb1 = _mla_attn(qt, kp, vt, batch=bs, length=ls, tok_off=tp, heads=mla_heads)
        x1, h2, ridx, rgate = _merge(ya, yb0, yb1, z, xs_in[0], xs_in[1], w_br_a[li].astype(BF16),
                                     w_br_b[li].astype(BF16), w_out[li].astype(BF16), g_moe[li][None, :],
                                     wr, br, ga_col=ga_col)

        dest, block_e, n_used, n_rows = _route(ridx[:, :TOP_K], n_exp)
        xs = _dispatch(dest, h2, n_rows)
        tf = _expert_tf(w_gate.shape[-1])
        ys = _experts(block_e, n_used, xs,
                      _tile_columns(w_gate[li].astype(BF16), tf), b_gate[li][:, None, :],
                      _tile_columns(w_up[li].astype(BF16), tf), b_up[li][:, None, :],
                      w_down[li].astype(BF16), b_down[li][:, None, :])
        final = li == depth - 1
        comb = functools.partial(_combine, dest, ys, x1, rgate, p, g_ple[li][None, :],
                                 w_ple_gate[li].astype(BF16), w_ple_proj[li].astype(BF16),
                                 g_final[None, :], final=final)
        outs = (comb(tok_off=0, n_tok=tp), comb(tok_off=tp, n_tok=ts))
        xs_in = outs
    return (outs[0].reshape(bp, lp, D), outs[1].reshape(bs, ls, D))
```

```python
import functools

import jax
import jax.numpy as jnp
from jax import lax
from jax.experimental import pallas as pl
from jax.experimental.pallas import tpu as pltpu

F32 = jnp.float32
BF16 = jnp.bfloat16
I32 = jnp.int32
U32 = jnp.uint32

LANE = 128
SUBLANE = 8
ROW_WORDS = SUBLANE * LANE
VMEM_LIMIT_BYTES = 56 * 1024 * 1024

RMS_EPS = 1e-6
GRID_W = 64
NA_KH = 8
NA_KW = 16
NA_DH = 128
NA_QROWS = 8
NA_QTOK = NA_QROWS * GRID_W
NA_WTOK = 2 * NA_QTOK
NA_PIECE = NA_WTOK // 4
NA_SM_STRIP = 128
NA_HEADS_PER_STEP = 8
NOPE = 128
ROPE = 64
V_DIM = 128
HEAD_PAD = 256
V_PAD = 144
ROPE_THETA = 10000.0
TOP_K = 4
SWIGLU_LIMIT = 7.0
SWIGLU_ALPHA = 1.702
MOE_BM = 1024
MLA_TQ_CANDS = (2048, 1024, 512)
MLA_TK_CANDS = (512, 256)
MLA_SM_STRIP = 256
LOG2E = 1.4426950408889634
NEG = -1e30


def _cparams(*sem):
    return pltpu.CompilerParams(dimension_semantics=sem, vmem_limit_bytes=VMEM_LIMIT_BYTES)


def _pick(n, cands):
    for c in cands:
        if n % c == 0:
            return c
    raise ValueError(f"no tile in {cands} divides {n}")


def _resident(shape):
    nd = len(shape)
    return pl.BlockSpec(shape, lambda *_: (0,) * nd, pipeline_mode=pl.Buffered(1))


def _rms(x, g):
    return x * lax.rsqrt(jnp.mean(x * x, axis=-1, keepdims=True) + RMS_EPS) * g


def _sigmoid(x):
    return 1.0 / (1.0 + jnp.exp(-x))


def _pack_pair(lo, hi):
    lo_bits = lax.bitcast_convert_type(lo.astype(BF16).astype(F32), U32) >> 16
    hi_bits = lax.bitcast_convert_type(hi.astype(BF16).astype(F32), U32)
    return lo_bits | hi_bits


def _unpack_pair(w):
    lo = lax.bitcast_convert_type(w << 16, F32)
    hi = lax.bitcast_convert_type(w & jnp.uint32(0xFFFF0000), F32)
    return lo, hi


def _store_packed_rows(ref, x):
    rows, d = x.shape
    assert d == 2 * ROW_WORDS
    for i in range(SUBLANE):
        lo = x[:, i * LANE:(i + 1) * LANE]
        hi = x[:, d // 2 + i * LANE:d // 2 + (i + 1) * LANE]
        ref[pl.ds(i, rows, stride=SUBLANE), :] = _pack_pair(lo, hi)


def _load_packed_rows(ref, rows):
    half = ROW_WORDS
    for i in range(SUBLANE):
        lo, hi = _unpack_pair(ref[pl.ds(i, rows, stride=SUBLANE), :])
        yield i * LANE, lo
        yield half + i * LANE, hi


def _group_specs(block, n_first, **kw):
    first = pl.BlockSpec(block, lambda i, *_: (jnp.minimum(i, n_first - 1), 0), **kw)
    second = pl.BlockSpec(block, lambda i, *_: (jnp.maximum(i - n_first, 0), 0), **kw)
    return [first, second]


def _inproj_kernel(xa_ref, xb_ref, g_ref, w_ref, wkr_ref, z_ref, kr_ref, h_scr, *, n_first):
    @pl.when(pl.program_id(1) == 0)
    def _():
        x = jnp.where(pl.program_id(0) < n_first, xa_ref[...], xb_ref[...])
        hb = _rms(x, g_ref[...]).astype(BF16)
        h_scr[...] = hb
        kr_ref[...] = jnp.dot(hb, wkr_ref[...], preferred_element_type=F32)

    z_ref[...] = jnp.dot(h_scr[...], w_ref[...], preferred_element_type=F32).astype(z_ref.dtype)


def _inproj(xa, xb, g, w, wkr):
    D = xa.shape[1]
    T = xa.shape[0] + xb.shape[0]
    N = w.shape[1]
    tm = _pick(xb.shape[0], (1024, 512, 256))
    assert xa.shape[0] % tm == 0
    tn = _pick(N, (1024, 512, 256, 128))
    n_first = xa.shape[0] // tm
    return pl.pallas_call(
        functools.partial(_inproj_kernel, n_first=n_first),
        grid=(T // tm, N // tn),
        in_specs=_group_specs((tm, D), n_first, pipeline_mode=pl.Buffered(1)) + [
                  pl.BlockSpec((1, D), lambda i, j: (0, 0)),
                  pl.BlockSpec((D, tn), lambda i, j: (0, j)),
                  pl.BlockSpec((D, LANE), lambda i, j: (0, 0))],
        out_specs=[pl.BlockSpec((tm, tn), lambda i, j: (i, j)),
                   pl.BlockSpec((tm, LANE), lambda i, j: (i, 0))],
        out_shape=[jax.ShapeDtypeStruct((T, N), BF16), jax.ShapeDtypeStruct((T, LANE), F32)],
        scratch_shapes=[pltpu.VMEM((tm, D), BF16)],
        compiler_params=_cparams("parallel", "arbitrary"),
        name="inproj",
    )(xa, xb, g, w, wkr)


def _mlaproj_kernel(cq_ref, ckv_ref, kr_ref, cos_ref, sin_ref, cost_ref, sint_ref, gq_ref, gkv_ref,
                    wqt_ref, wk_ref, wvt_ref, qt_out, k_out, vt_out, *, heads):
    tm = cq_ref.shape[0]
    cqn = _rms(cq_ref[...].astype(F32), gq_ref[...]).astype(BF16)
    ckvn = _rms(ckv_ref[...].astype(F32), gkv_ref[...]).astype(BF16)
    ckvn_t = ckvn.T
    qt = jnp.dot(wqt_ref[...], cqn.T, preferred_element_type=F32)
    kn = jnp.dot(ckvn, wk_ref[...], preferred_element_type=F32)
    vt = jnp.dot(wvt_ref[...], ckvn_t, preferred_element_type=F32)
    kr = kr_ref[...]
    krope = (kr * cos_ref[...] + pltpu.roll(kr, ROPE, axis=1) * sin_ref[...]).astype(BF16)
    cos_t = cost_ref[...]
    sin_t = sint_ref[...]
    tail = (lax.broadcasted_iota(I32, (V_PAD - V_DIM, tm), 0) == 0).astype(BF16)
    for h in range(heads):
        c = h * HEAD_PAD
        qt_out[c:c + NOPE, :] = qt[c:c + NOPE].astype(BF16)
        x = qt[c + NOPE:c + NOPE + ROPE]
        x_swapped = qt[c + NOPE + ROPE:c + HEAD_PAD]
        qt_out[c + NOPE:c + NOPE + ROPE, :] = (x * cos_t + x_swapped * sin_t).astype(BF16)
        qt_out[c + NOPE + ROPE:c + HEAD_PAD, :] = jnp.zeros((ROPE, tm), BF16)
        k_out[:, c:c + NOPE] = kn[:, h * NOPE:(h + 1) * NOPE].astype(BF16)
        k_out[:, c + NOPE:c + HEAD_PAD] = krope
        vt_out[h * V_PAD:h * V_PAD + V_DIM, :] = vt[h * V_DIM:(h + 1) * V_DIM].astype(BF16)
        vt_out[h * V_PAD + V_DIM:(h + 1) * V_PAD, :] = tail


def _mlaproj(z, kr, tabs, gq, gkv, wqt, wk, wvt, *, cq_col, groups, heads):
    T = z.shape[0]
    ql = gq.shape[-1]
    kvl = gkv.shape[-1]
    W = heads * HEAD_PAD
    (bp, lp), (bs, ls) = groups
    tm = _pick(ls, (256, 128))
    n_p = bp * lp // tm
    cos_n, sin_n, cos_t, sin_t = tabs

    def pos_blk(i):
        return jnp.where(i < n_p, i % (lp // tm), (i - n_p) % (ls // tm))

    return pl.pallas_call(
        functools.partial(_mlaproj_kernel, heads=heads),
        grid=(T // tm,),
        in_specs=[pl.BlockSpec((tm, ql), lambda i: (i, cq_col // ql)),
                  pl.BlockSpec((tm, kvl), lambda i: (i, (cq_col + ql) // kvl)),
                  pl.BlockSpec((tm, LANE), lambda i: (i, 0)),
                  pl.BlockSpec((tm, LANE), lambda i: (pos_blk(i), 0)),
                  pl.BlockSpec((tm, LANE), lambda i: (pos_blk(i), 0)),
                  pl.BlockSpec((ROPE, tm), lambda i: (0, pos_blk(i))),
                  pl.BlockSpec((ROPE, tm), lambda i: (0, pos_blk(i))),
                  _resident((1, ql)), _resident((1, kvl)),
                  _resident(wqt.shape), _resident(wk.shape), _resident(wvt.shape)],
        out_specs=[pl.BlockSpec((W, tm), lambda i: (0, i)),
                   pl.BlockSpec((tm, W), lambda i: (i, 0)),
                   pl.BlockSpec((heads * V_PAD, tm), lambda i: (0, i))],
        out_shape=[jax.ShapeDtypeStruct((W, T), BF16), jax.ShapeDtypeStruct((T, W), BF16),
                   jax.ShapeDtypeStruct((heads * V_PAD, T), BF16)],
        compiler_params=_cparams("parallel"),
        name="mlaproj",
    )(z, z, kr, cos_n, sin_n, cos_t, sin_t, gq, gkv, wqt, wk, wvt)


def _mla_attn_kernel(q_ref, k_ref, v_ref, o_ref, s0, s1, p0, p1, a0, a1, m_scr, acc_scr, *, tk, nk):
    qt = q_ref[...]

    def qk(c, s_ref):
        off = pl.multiple_of(c * tk, tk)
        s_ref[...] = jnp.dot(k_ref[pl.ds(off, tk), :], qt, preferred_element_type=F32)

    def sm(s_ref, p_ref, a_ref):
        for j in range(s_ref.shape[1] // MLA_SM_STRIP):
            sl = slice(j * MLA_SM_STRIP, (j + 1) * MLA_SM_STRIP)
            m_prev = m_scr[:, sl]
            m_new = jnp.maximum(m_prev, jnp.max(s_ref[:, sl], axis=0, keepdims=True))
            a_ref[:, sl] = jnp.exp2(m_prev - m_new)
            p_ref[:, sl] = jnp.exp2(s_ref[:, sl] - m_new).astype(BF16)
            m_scr[:, sl] = m_new

    def pv(c, p_ref, a_ref):
        off = pl.multiple_of(c * tk, tk)
        acc_scr[...] = acc_scr[...] * a_ref[...] + jnp.dot(v_ref[:, pl.ds(off, tk)], p_ref[...],
                                                           preferred_element_type=F32)

    m_scr[...] = jnp.full(m_scr.shape, NEG, F32)
    acc_scr[...] = jnp.zeros(acc_scr.shape, F32)
    qk(0, s0)
    qk(1, s1)
    sm(s0, p0, a0)

    def body(j, carry):
        qk(2 * j, s0)
        sm(s1, p1, a1)
        pv(2 * j - 2, p0, a0)
        qk(2 * j + 1, s1)
        sm(s0, p0, a0)
        pv(2 * j - 1, p1, a1)
        return carry

    lax.fori_loop(1, nk // 2, body, 0)
    sm(s1, p1, a1)
    pv(nk - 2, p0, a0)
    pv(nk - 1, p1, a1)
    acc = acc_scr[...]
    o_ref[...] = (acc[:V_DIM] / acc[V_DIM:V_DIM + 1]).T.astype(o_ref.dtype)


def _mla_attn(qt, kp, vt, *, batch, length, tok_off, heads):
    tq = _pick(length, MLA_TQ_CANDS)
    tk = _pick(length // 2, MLA_TK_CANDS)
    nk = length // tk
    assert nk % 2 == 0 and tok_off % length == 0
    nq = length // tq
    qoff = tok_off // tq
    boff = tok_off // length
    return pl.pallas_call(
        functools.partial(_mla_attn_kernel, tk=tk, nk=nk),
        grid=(batch, heads, nq),
        in_specs=[pl.BlockSpec((HEAD_PAD, tq), lambda b, h, i: (h, qoff + b * nq + i)),
                  pl.BlockSpec((length, HEAD_PAD), lambda b, h, i: (boff + b, h)),
                  pl.BlockSpec((V_PAD, length), lambda b, h, i: (h, boff + b))],
        out_specs=pl.BlockSpec((tq, V_DIM), lambda b, h, i: (b * nq + i, h)),
        out_shape=jax.ShapeDtypeStruct((batch * length, heads * V_DIM), BF16),
        scratch_shapes=[pltpu.VMEM((tk, tq), F32), pltpu.VMEM((tk, tq), F32),
                        pltpu.VMEM((tk, tq), BF16), pltpu.VMEM((tk, tq), BF16),
                        pltpu.VMEM((1, tq), F32), pltpu.VMEM((1, tq), F32),
                        pltpu.VMEM((1, tq), F32), pltpu.VMEM((V_PAD, tq), F32)],
        compiler_params=_cparams("parallel", "parallel", "arbitrary"),
        name="mla_attn",
    )(qt, kp, vt)


def _na_kernel(q_ref, k0, k1, k2, k3, v0, v1, v2, v3, b_ref, o_ref, s_scr, p_scr, den_scr):
    pw = NA_PIECE
    for hh in range(NA_HEADS_PER_STEP):
        cols = slice(hh * NA_DH, (hh + 1) * NA_DH)
        qt = q_ref[:, cols].T
        for j, kr in enumerate((k0, k1, k2, k3)):
            s_scr[j * pw:(j + 1) * pw, :] = (jnp.dot(kr[:, cols], qt, preferred_element_type=F32)
                                             + b_ref[hh, j * pw:(j + 1) * pw, :])
        for t in range(q_ref.shape[0] // NA_SM_STRIP):
            sl = slice(t * NA_SM_STRIP, (t + 1) * NA_SM_STRIP)
            m = jnp.max(s_scr[:, sl], axis=0, keepdims=True)
            p = jnp.exp2(s_scr[:, sl] - m)
            den_scr[:, sl] = jnp.sum(p, axis=0, keepdims=True)
            p_scr[:, sl] = p.astype(BF16)
        acc = None
        for j, vr in enumerate((v0, v1, v2, v3)):
            a = jnp.dot(vr[:, cols].T, p_scr[j * pw:(j + 1) * pw, :], preferred_element_type=F32)
            acc = a if acc is None else acc + a
        o_ref[:, cols] = (acc / den_scr[...]).T.astype(o_ref.dtype)


def _na_bias(rpb):
    H = rpb.shape[0]
    a = jnp.arange(NA_QROWS)
    j = jnp.arange(2 * NA_QROWS)
    c = jnp.arange(GRID_W)
    q_rel = jnp.stack([a, a + NA_KH // 2, a + NA_QROWS])
    rs_rel = jnp.stack([jnp.maximum(a - NA_KH // 2, 0), a, jnp.minimum(a + NA_KH // 2, NA_QROWS)])
    vrow = (j[None, None, :] >= rs_rel[:, :, None]) & (j[None, None, :] < rs_rel[:, :, None] + NA_KH)
    dy = jnp.clip(j[None, None, :] - q_rel[:, :, None] + NA_KH - 1, 0, 2 * NA_KH - 2)
    cs = jnp.clip(c - NA_KW // 2, 0, GRID_W - NA_KW)
    vcol = (c[None, :] >= cs[:, None]) & (c[None, :] < cs[:, None] + NA_KW)
    dx = jnp.clip(c[None, :] - c[:, None] + NA_KW - 1, 0, 2 * NA_KW - 2)
    oh_x = (dx[:, :, None] == jnp.arange(2 * NA_KW - 1)[None, None, :]).astype(F32)
    oh_y = (dy[..., None] == jnp.arange(2 * NA_KH - 1)[None, None, None, :]).astype(F32)
    b2 = jnp.einsum("hyx,cdx->hycd", rpb.astype(F32), oh_x, precision=lax.Precision.HIGHEST)
    b = jnp.einsum("kajy,hycd->hkacjd", oh_y, b2, precision=lax.Precision.HIGHEST)
    valid = vrow[:, :, None, :, None] & vcol[None, None, :, None, :]
    b = jnp.where(valid[None], b * LOG2E, NEG)
    return b.transpose(1, 0, 4, 5, 2, 3).reshape(3, H, NA_WTOK, NA_QTOK)


def _na_attn(z, bias, *, groups, heads, k_col, v_col):
    T = z.shape[0]
    (bp, lp), (bs, ls) = groups
    n_p = bp * lp // NA_QTOK
    nbp = lp // NA_QTOK
    nbs = ls // NA_QTOK
    assert nbp >= 2 and nbs >= 2, "needs at least 16 grid rows per sequence"
    piece_per_blk = NA_QTOK // NA_PIECE

    def meta(g):
        is_p = g < n_p
        gl = jnp.where(is_p, g, g - n_p)
        nb = jnp.where(is_p, nbp, nbs)
        seq = gl // nb
        i = gl % nb
        base = jnp.where(is_p, 0, bp * lp // NA_PIECE) + seq * (piece_per_blk * nb)
        w0 = jnp.clip(piece_per_blk * i - 1, 0, piece_per_blk * nb - 4)
        kind = jnp.where(i == 0, 0, jnp.where(i == nb - 1, 2, 1))
        return base + w0, kind

    hb = NA_HEADS_PER_STEP
    wcols = hb * NA_DH
    assert heads % hb == 0 and k_col % wcols == 0 and v_col % wcols == 0

    def piece_spec(col, jj):
        return pl.BlockSpec((NA_PIECE, wcols), lambda h, g: (meta(g)[0] + jj, col + h))

    return pl.pallas_call(
        _na_kernel,
        grid=(heads // hb, T // NA_QTOK),
        in_specs=([pl.BlockSpec((NA_QTOK, wcols), lambda h, g: (g, h))]
                  + [piece_spec(k_col // wcols, jj) for jj in range(4)]
                  + [piece_spec(v_col // wcols, jj) for jj in range(4)]
                  + [pl.BlockSpec((None, hb, NA_WTOK, NA_QTOK), lambda h, g: (meta(g)[1], h, 0, 0))]),
        out_specs=pl.BlockSpec((NA_QTOK, wcols), lambda h, g: (g, h)),
        out_shape=jax.ShapeDtypeStruct((T, heads * NA_DH), BF16),
        scratch_shapes=[pltpu.VMEM((NA_WTOK, NA_QTOK), F32), pltpu.VMEM((NA_WTOK, NA_QTOK), BF16),
                        pltpu.VMEM((1, NA_QTOK), F32)],
        compiler_params=_cparams("parallel", "parallel"),
        name="na_attn",
    )(z, *([z] * 8), bias)


def _merge_kernel(ya_ref, yb0_ref, yb1_ref, ga_ref, gb_ref, x0_ref, x1in_ref, wa_ref, wb_ref, wo_ref, gm_ref,
                  wr_ref, br_ref, x1_ref, h2_ref, ridx_ref, rgate_ref, *, n_first):
    in_first = pl.program_id(0) < n_first
    ya = jnp.dot(ya_ref[...], wa_ref[...], preferred_element_type=F32)
    yb = jnp.dot(jnp.where(in_first, yb0_ref[...], yb1_ref[...]), wb_ref[...], preferred_element_type=F32)
    u = _sigmoid(ga_ref[...].astype(F32)) * ya + _sigmoid(gb_ref[...].astype(F32)) * yb
    x = jnp.where(in_first, x0_ref[...], x1in_ref[...])
    x1 = x + jnp.dot(u.astype(BF16), wo_ref[...], preferred_element_type=F32)
    x1_ref[...] = x1
    h2 = _rms(x1, gm_ref[...])
    _store_packed_rows(h2_ref, h2)
    logits = jnp.dot(h2.astype(BF16), wr_ref[...], preferred_element_type=F32) + br_ref[...]
    lane = lax.broadcasted_iota(I32, logits.shape, 1)
    vals, idxs = [], []
    for _ in range(TOP_K):
        mx = jnp.max(logits, axis=-1, keepdims=True)
        ix = jnp.min(jnp.where(logits == mx, lane, LANE), axis=-1, keepdims=True)
        vals.append(mx)
        idxs.append(ix)
        logits = jnp.where(lane == ix, 2 * NEG, logits)
    es = [jnp.exp(v - vals[0]) for v in vals]
    den = functools.reduce(lambda a, b: a + b, es)
    ridx = jnp.zeros(lane.shape, I32)
    rgate = jnp.zeros(lane.shape, F32)
    for k in range(TOP_K):
        ridx = jnp.where(lane == k, idxs[k], ridx)
        rgate = jnp.where(lane == k, es[k] / den, rgate)
    ridx_ref[...] = ridx
    rgate_ref[...] = rgate


def _merge(ya, yb0, yb1, z, x0, x1, wa, wb, wo, gm, wr, br, *, ga_col):
    D = x0.shape[1]
    T = x0.shape[0] + x1.shape[0]
    tm = 256
    n_first = x0.shape[0] // tm
    return pl.pallas_call(
        functools.partial(_merge_kernel, n_first=n_first),
        grid=(T // tm,),
        in_specs=[pl.BlockSpec((tm, D), lambda i: (i, 0))]
                 + _group_specs((tm, D), n_first)
                 + [pl.BlockSpec((tm, D), lambda i: (i, ga_col // D)),
                    pl.BlockSpec((tm, D), lambda i: (i, ga_col // D + 1))]
                 + _group_specs((tm, D), n_first)
                 + [_resident(wa.shape), _resident(wb.shape), _resident(wo.shape),
                    _resident((1, D)), _resident((D, LANE)), _resident((1, LANE))],
        out_specs=[pl.BlockSpec((tm, D), lambda i: (i, 0)),
                   pl.BlockSpec((tm * SUBLANE, LANE), lambda i: (i, 0)),
                   pl.BlockSpec((tm, LANE), lambda i: (i, 0)),
                   pl.BlockSpec((tm, LANE), lambda i: (i, 0))],
        out_shape=[jax.ShapeDtypeStruct((T, D), F32), jax.ShapeDtypeStruct((T * SUBLANE, LANE), U32),
                   jax.ShapeDtypeStruct((T, LANE), I32), jax.ShapeDtypeStruct((T, LANE), F32)],
        compiler_params=_cparams("parallel"),
        name="merge_router",
    )(ya, yb0, yb1, z, z, x0, x1, wa, wb, wo, gm, wr, br)


def _row_tile(ref, r):
    return ref.at[pl.ds(pl.multiple_of(r * SUBLANE, SUBLANE), SUBLANE), :]


def _dispatch_kernel(dest_ref, h_ref, xs_in_ref, xs_ref, sem, *, tm):
    del xs_in_ref

    def body(r, c):
        for k in range(TOP_K):
            pltpu.make_async_copy(_row_tile(h_ref, r), _row_tile(xs_ref, dest_ref[TOP_K * r + k]), sem).start()
        return c

    lax.fori_loop(0, tm, body, 0)
    for _ in range(TOP_K):
        pltpu.make_async_copy(h_ref, xs_ref.at[pl.ds(0, tm * SUBLANE), :], sem).wait()


def _dispatch(dest, h2p, n_rows):
    T = h2p.shape[0] // SUBLANE
    tm = 256
    xs0 = jnp.zeros((n_rows * SUBLANE, LANE), U32)
    return pl.pallas_call(
        functools.partial(_dispatch_kernel, tm=tm),
        grid=(T // tm,),
        in_specs=[pl.BlockSpec((TOP_K * tm,), lambda i: (i,), memory_space=pltpu.SMEM),
                  pl.BlockSpec((tm * SUBLANE, LANE), lambda i: (i, 0)),
                  pl.BlockSpec(memory_space=pl.ANY)],
        out_specs=pl.BlockSpec(memory_space=pl.ANY),
        out_shape=jax.ShapeDtypeStruct((n_rows * SUBLANE, LANE), U32),
        scratch_shapes=[pltpu.SemaphoreType.DMA],
        input_output_aliases={2: 0},
        compiler_params=_cparams("arbitrary"),
        name="moe_dispatch",
    )(dest, h2p, xs0)


def _expert_kernel(be_ref, nu_ref, xs_ref, wg_ref, bg_ref, wu_ref, bu_ref, wd_ref, bd_ref, ys_ref,
                   xb_scr, acc_scr, *, nf):
    i = pl.program_id(0)
    f = pl.program_id(1)
    bm = xb_scr.shape[0]
    used = i < nu_ref[0]

    @pl.when(used)
    def _():
        @pl.when(f == 0)
        def _():
            for col, blk in _load_packed_rows(xs_ref, bm):
                xb_scr[:, col:col + LANE] = blk.astype(BF16)
            acc_scr[...] = jnp.broadcast_to(bd_ref[0], acc_scr.shape)

        xb = xb_scr[...]
        g = jnp.dot(xb, wg_ref[0], preferred_element_type=F32) + bg_ref[0]
        u = jnp.dot(xb, wu_ref[0], preferred_element_type=F32) + bu_ref[0]
        g = jnp.minimum(g, SWIGLU_LIMIT)
        u = jnp.clip(u, -SWIGLU_LIMIT, SWIGLU_LIMIT)
        a = g * _sigmoid(SWIGLU_ALPHA * g) * (u + 1.0)
        acc_scr[...] += jnp.dot(a.astype(BF16), wd_ref[0], preferred_element_type=F32)

    @pl.when(f == nf - 1)
    def _():
        @pl.when(used)
        def _():
            _store_packed_rows(ys_ref, acc_scr[...])

        @pl.when(jnp.logical_not(used))
        def _():
            ys_ref[...] = jnp.zeros(ys_ref.shape, U32)


def _experts(block_e, n_used, xs, wg, bg, wu, bu, wd, bd):
    P = xs.shape[0] // SUBLANE
    E, D, F = wg.shape
    tf = _pick(F, (512, 256, 128))
    nf = F // tf
    nblk = P // MOE_BM

    def blk(i, nu):
        return jnp.minimum(i, nu[0] - 1)

    def fidx(i, f, nu):
        return jnp.where(i < nu[0], f, nf - 1)

    grid_spec = pltpu.PrefetchScalarGridSpec(
        num_scalar_prefetch=2,
        grid=(nblk, nf),
        in_specs=[pl.BlockSpec((MOE_BM * SUBLANE, LANE), lambda i, f, be, nu: (blk(i, nu), 0)),
                  pl.BlockSpec((1, D, tf), lambda i, f, be, nu: (be[blk(i, nu)], 0, fidx(i, f, nu))),
                  pl.BlockSpec((1, 1, tf), lambda i, f, be, nu: (be[blk(i, nu)], 0, fidx(i, f, nu))),
                  pl.BlockSpec((1, D, tf), lambda i, f, be, nu: (be[blk(i, nu)], 0, fidx(i, f, nu))),
                  pl.BlockSpec((1, 1, tf), lambda i, f, be, nu: (be[blk(i, nu)], 0, fidx(i, f, nu))),
                  pl.BlockSpec((1, tf, D), lambda i, f, be, nu: (be[blk(i, nu)], fidx(i, f, nu), 0)),
                  pl.BlockSpec((1, 1, D), lambda i, f, be, nu: (be[blk(i, nu)], 0, 0))],
        out_specs=pl.BlockSpec((MOE_BM * SUBLANE, LANE), lambda i, f, be, nu: (i, 0)),
        scratch_shapes=[pltpu.VMEM((MOE_BM, D), BF16), pltpu.VMEM((MOE_BM, D), F32)],
    )
    return pl.pallas_call(
        functools.partial(_expert_kernel, nf=nf),
        grid_spec=grid_spec,
        out_shape=jax.ShapeDtypeStruct((P * SUBLANE, LANE), U32),
        compiler_params=_cparams("arbitrary", "arbitrary"),
        name="moe_experts",
    )(block_e, n_used, xs, wg, bg, wu, bu, wd, bd)


def _combine_kernel(pos_ref, ys_ref, x1_ref, gate_ref, p_ref, gp_ref, wpg_ref, wpp_ref, gf_ref, o_ref,
                    b0, b1, b2, b3, x2_scr, sem, *, tm, final):
    bufs = (b0, b1, b2, b3)

    def body(r, c):
        for k in range(TOP_K):
            pltpu.make_async_copy(_row_tile(ys_ref, pos_ref[TOP_K * r + k]), _row_tile(bufs[k], r), sem).start()
        return c

    lax.fori_loop(0, tm, body, 0)
    for k in range(TOP_K):
        pltpu.make_async_copy(ys_ref.at[pl.ds(0, tm * SUBLANE), :], bufs[k], sem).wait()

    gates = gate_ref[...]
    x2_scr[...] = x1_ref[...]
    for k in range(TOP_K):
        g = gates[:, k:k + 1]
        for col, blk in _load_packed_rows(bufs[k], tm):
            x2_scr[:, col:col + LANE] += g * blk
    x2 = x2_scr[...]
    h3 = _rms(x2, gp_ref[...]).astype(BF16)
    gt = _sigmoid(jnp.dot(h3, wpg_ref[...], preferred_element_type=F32))
    pe = jnp.dot(p_ref[...].astype(BF16), wpp_ref[...], preferred_element_type=F32)
    x3 = x2 + gt * pe
    o_ref[...] = _rms(x3, gf_ref[...]) if final else x3


def _combine(pos, ys, x1, gates, p, gp, wpg, wpp, gf, *, tok_off, n_tok, final):
    D = x1.shape[1]
    tm = 256
    off = tok_off // tm
    return pl.pallas_call(
        functools.partial(_combine_kernel, tm=tm, final=final),
        grid=(n_tok // tm,),
        in_specs=[pl.BlockSpec((TOP_K * tm,), lambda i: (off + i,), memory_space=pltpu.SMEM),
                  pl.BlockSpec(memory_space=pl.ANY),
                  pl.BlockSpec((tm, D), lambda i: (off + i, 0)),
                  pl.BlockSpec((tm, LANE), lambda i: (off + i, 0)),
                  pl.BlockSpec((tm, p.shape[1]), lambda i: (off + i, 0)),
                  _resident((1, D)), _resident(wpg.shape), _resident(wpp.shape), _resident((1, D))],
        out_specs=pl.BlockSpec((tm, D), lambda i: (i, 0)),
        out_shape=jax.ShapeDtypeStruct((n_tok, D), F32),
        scratch_shapes=[pltpu.VMEM((tm * SUBLANE, LANE), U32)] * TOP_K
                       + [pltpu.VMEM((tm, D), F32), pltpu.SemaphoreType.DMA],
        compiler_params=_cparams("arbitrary"),
        name="moe_combine_ple",
    )(pos, ys, x1, gates, p, gp, wpg, wpp, gf)


def _route(ridx, n_exp):
    flat_e = ridx.reshape(-1)
    tk = flat_e.shape[0]
    oh = (flat_e[:, None] == jnp.arange(n_exp, dtype=I32)[None, :]).astype(I32)
    csum = jnp.cumsum(oh, axis=0)
    rank = jnp.sum(oh * csum, axis=1) - 1
    counts = csum[-1]
    pcounts = (counts + MOE_BM - 1) // MOE_BM * MOE_BM
    pend = jnp.cumsum(pcounts)
    pstart = pend - pcounts
    dest = (pstart[flat_e] + rank).astype(I32)
    nblk = (tk + n_exp * (MOE_BM - 1) + MOE_BM - 1) // MOE_BM
    blk_start = jnp.arange(nblk, dtype=I32) * MOE_BM
    block_e = jnp.minimum(jnp.sum((blk_start[:, None] >= pend[None, :]).astype(I32), axis=1), n_exp - 1)
    n_used = (pend[-1:] // MOE_BM).astype(I32)
    return dest, block_e, n_used, nblk * MOE_BM


def _swap_halves(w):
    half = w.shape[-1] // 2
    return jnp.concatenate([w[..., half:], w[..., :half]], axis=-1)


def _rope_tables(length):
    pos = jnp.arange(length, dtype=F32)
    inv_freq = ROPE_THETA ** (-jnp.arange(0, ROPE, 2, dtype=F32) / ROPE)
    ang = pos[:, None] * inv_freq[None, :]
    cos, sin = jnp.cos(ang), jnp.sin(ang)
    cos2 = jnp.concatenate([cos, cos], axis=-1)
    sin2 = jnp.concatenate([-sin, sin], axis=-1)
    zeros = jnp.zeros((length, LANE - ROPE), F32)
    return (jnp.concatenate([cos2, zeros], axis=-1), jnp.concatenate([sin2, zeros], axis=-1),
            cos2.T, sin2.T)


def kernel(x_prompt, x_sample, p_prompt, p_sample, g_mix, w_in, g_q_lat, w_q_up, g_kv_lat, w_kv_up, na_rpb, w_br_a, w_br_b, w_out, g_moe, w_router, b_router, w_gate, b_gate, w_up, b_up, w_down, b_down, g_ple, w_ple_gate, w_ple_proj, g_final):
    bp, lp, D = x_prompt.shape
    bs, ls, _ = x_sample.shape
    tp, ts = bp * lp, bs * ls
    groups = ((bp, lp), (bs, ls))
    depth = w_in.shape[0]
    na_heads = na_rpb.shape[1]
    na_w = na_heads * NA_DH
    ql = g_q_lat.shape[-1]
    kvl = g_kv_lat.shape[-1]
    mla_heads = w_q_up.shape[-1] // (NOPE + ROPE)
    n_exp = w_router.shape[-1]
    assert na_w == D and ql == kvl and lp >= ls

    xs_in = (x_prompt.reshape(tp, D), x_sample.reshape(ts, D))
    rope_tabs = _rope_tables(lp)
    outs = None
    for li in range(depth):
        p = jnp.concatenate([p_prompt[li].reshape(tp, -1), p_sample[li].reshape(ts, -1)], axis=0)
        wi = w_in[li]
        o = [0]
        for wdt in (na_w, na_w, na_w, ql, kvl, ROPE, D, D):
            o.append(o[-1] + wdt)
        w_qa, w_ka, w_va, w_cq, w_ckv, w_kr, w_ga, w_gb = (wi[:, o[k]:o[k + 1]] for k in range(8))
        w_main = jnp.concatenate([w_qa * (NA_DH ** -0.5 * LOG2E), w_ka, w_va, w_ga, w_gb, w_cq, w_ckv],
                                 axis=1).astype(BF16)
        ga_col = 3 * na_w
        cq_col = 3 * na_w + 2 * D
        w_kr2 = jnp.concatenate([w_kr, _swap_halves(w_kr)], axis=1).astype(BF16)
        wq3 = w_q_up[li].reshape(ql, mla_heads, NOPE + ROPE) * ((NOPE + ROPE) ** -0.5 * LOG2E)
        wqt = jnp.concatenate([wq3[..., :NOPE], wq3[..., NOPE:], _swap_halves(wq3[..., NOPE:])],
                              axis=-1).reshape(ql, mla_heads * HEAD_PAD).T.astype(BF16)
        wkv3 = w_kv_up[li].reshape(kvl, mla_heads, NOPE + V_DIM)
        wk = wkv3[..., :NOPE].reshape(kvl, mla_heads * NOPE).astype(BF16)
        wvt = wkv3[..., NOPE:].reshape(kvl, mla_heads * V_DIM).T.astype(BF16)
        bias = _na_bias(na_rpb[li])
        wr = jnp.pad(w_router[li], ((0, 0), (0, LANE - n_exp))).astype(BF16)
        br = jnp.pad(b_router[li].astype(F32), (0, LANE - n_exp), constant_values=NEG)[None, :]

        z, kr = _inproj(xs_in[0], xs_in[1], g_mix[li][None, :], w_main, w_kr2)
        qt, kp, vt = _mlaproj(z, kr, rope_tabs, g_q_lat[li][None, :], g_kv_lat[li][None, :], wqt, wk, wvt,
                              cq_col=cq_col, groups=groups, heads=mla_heads)
        ya = _na_attn(z, bias, groups=groups, heads=na_heads, k_col=na_w, v_col=2 * na_w)
        yb0 = _mla_attn(qt, kp, vt, batch=bp, length=lp, tok_off=0, heads=mla_heads)
        yb1 = _mla_attn(qt, kp, vt, batch=bs, length=ls, tok_off=tp, heads=mla_heads)
        x1, h2, ridx, rgate = _merge(ya, yb0, yb1, z, xs_in[0], xs_in[1], w_br_a[li].astype(BF16),
                                     w_br_b[li].astype(BF16), w_out[li].astype(BF16), g_moe[li][None, :],
                                     wr, br, ga_col=ga_col)

        dest, block_e, n_used, n_rows = _route(ridx[:, :TOP_K], n_exp)
        xs = _dispatch(dest, h2, n_rows)
        ys = _experts(block_e, n_used, xs,
                      w_gate[li].astype(BF16), b_gate[li][:, None, :], w_up[li].astype(BF16),
                      b_up[li][:, None, :], w_down[li].astype(BF16), b_down[li][:, None, :])
        final = li == depth - 1
        comb = functools.partial(_combine, dest, ys, x1, rgate, p, g_ple[li][None, :],
                                 w_ple_gate[li].astype(BF16), w_ple_proj[li].astype(BF16),
                                 g_final[None, :], final=final)
        outs = (comb(tok_off=0, n_tok=tp), comb(tok_off=tp, n_tok=ts))
        xs_in = outs
    return (outs[0].reshape(bp, lp, D), outs[1].reshape(bs, ls, D))
```
